```python
import math
import jax, jax.numpy as jnp
from jax import lax
import numpy as np

D_MODEL = 1024
BATCH = 8
SEQ = 4096
DEPTH = 2

N_A_LAYERS = DEPTH // 2
N_B_LAYERS = DEPTH - N_A_LAYERS
HGRN_HEADS = 8
HGRN_DK = D_MODEL // HGRN_HEADS
HGRN_DV = D_MODEL // HGRN_HEADS
HGRN_CHUNK = 64
DIFF_HEADS = 8
DIFF_DH = D_MODEL // (2 * DIFF_HEADS)
DIFF_DV = 2 * DIFF_DH
Q_BLOCK = 128
N_EXPERTS = 16
N_GROUPS = 4
EXPERTS_PER_GROUP = N_EXPERTS // N_GROUPS
TOP_K = 2
D_EXPERT = 512
MOE_BLOCK = 128
DEEPNORM_ALPHA = (2.0 * DEPTH) ** 0.25
DEEPNORM_BETA = (8.0 * DEPTH) ** -0.25
LN_EPS = 1e-5
RMS_EPS = 1e-6

kernel_name = "yoco_hgrn2_diffattn_grouped_moe"


def layer_norm(x, g, b):
    xf = x.astype(jnp.float32)
    mu = jnp.mean(xf, axis=-1, keepdims=True)
    var = jnp.mean(jnp.square(xf - mu), axis=-1, keepdims=True)
    return ((xf - mu) * lax.rsqrt(var + LN_EPS) * g + b).astype(x.dtype)


def rms_norm(xf, g):
    return xf * lax.rsqrt(jnp.mean(jnp.square(xf), axis=-1, keepdims=True) + RMS_EPS) * g


def hgrn2_mixer(x, w_in, lb, norm_g, w_out):
    B, S, _ = x.shape
    H, DK, DV, C = HGRN_HEADS, HGRN_DK, HGRN_DV, HGRN_CHUNK
    f32 = jnp.float32
    proj = jnp.einsum('bsd,de->bse', x, w_in)
    q, fl, v, g = jnp.split(proj, 4, axis=-1)
    f = lb + (1.0 - lb) * jax.nn.sigmoid(fl.astype(f32))
    log_f = jnp.log(f)
    k = 1.0 - f

    def to_chunks(t):
        return t.astype(f32).reshape(B, S // C, C, H, t.shape[-1] // H).transpose(1, 0, 3, 2, 4)

    causal = jnp.tril(jnp.ones((C, C), dtype=bool))

    def step(state, inp):
        qc, kc, vc, lfc = inp
        b = jnp.cumsum(lfc, axis=2)
        ref = b[:, :, C // 2 - 1:C // 2, :]
        scores = jnp.einsum('bhtd,bhsd->bhts', qc * jnp.exp(b - ref), kc * jnp.exp(ref - b))
        scores = jnp.where(causal, scores, 0.0)
        o = (jnp.einsum('bhts,bhsv->bhtv', scores, vc)
             + jnp.einsum('bhtd,bhdv->bhtv', qc * jnp.exp(b), state))
        b_last = b[:, :, -1:, :]
        new_state = (jnp.exp(b_last)[:, :, 0, :, None] * state
                     + jnp.einsum('bhsd,bhsv->bhdv', kc * jnp.exp(b_last - b), vc))
        return new_state, o

    state0 = jnp.zeros((B, H, DK, DV), f32)
    _, o = lax.scan(step, state0, (to_chunks(q), to_chunks(k), to_chunks(v), to_chunks(log_f)))
    o = o.transpose(1, 0, 3, 2, 4).reshape(B, S, H, DV)
    o = rms_norm(o, norm_g).reshape(B, S, H * DV) * jax.nn.sigmoid(g.astype(f32))
    return jnp.einsum('bse,ed->bsd', o.astype(x.dtype), w_out)


def shared_kv(x, kv_w):
    B, S, _ = x.shape
    kv = jnp.einsum('bsd,de->bse', x, kv_w)
    kpart, vpart = jnp.split(kv, [2 * DIFF_HEADS * DIFF_DH], axis=-1)
    k = kpart.reshape(B, S, DIFF_HEADS, 2, DIFF_DH).transpose(0, 2, 3, 1, 4)
    v = vpart.reshape(B, S, DIFF_HEADS, DIFF_DV).transpose(0, 2, 1, 3)
    return k, v


def diff_attention(x, k, v, w_q, lam, subln_g, w_out, lambda_init):
    B, S, _ = x.shape
    H, DH, DV = DIFF_HEADS, DIFF_DH, DIFF_DV
    NB = S // Q_BLOCK
    q = jnp.einsum('bsd,de->bse', x, w_q).reshape(B, S, H, 2, DH).transpose(0, 2, 3, 1, 4)
    q = q * (DH ** -0.5)
    lamf = lam.astype(jnp.float32)
    lam_val = (jnp.exp(jnp.sum(lamf[0] * lamf[1])) - jnp.exp(jnp.sum(lamf[2] * lamf[3]))
               + lambda_init)
    q_blocks = q.reshape(B, H, 2, NB, Q_BLOCK, DH).transpose(3, 0, 1, 2, 4, 5)
    kpos = jnp.arange(S)

    def attend(inp):
        qb, bi = inp
        qpos = bi * Q_BLOCK + jnp.arange(Q_BLOCK)
        s = jnp.einsum('bhcqd,bhckd->bhcqk', qb, k).astype(jnp.float32)
        s = jnp.where(kpos[None, :] <= qpos[:, None], s, -jnp.inf)
        p = jax.nn.softmax(s, axis=-1)
        a = p[:, :, 0] - lam_val * p[:, :, 1]
        return jnp.einsum('bhqk,bhkv->bhqv', a.astype(v.dtype), v)

    o = lax.map(attend, (q_blocks, jnp.arange(NB)))
    o = o.transpose(1, 0, 3, 2, 4).reshape(B, S, H, DV).astype(jnp.float32)
    o = rms_norm(o, subln_g) * (1.0 - lambda_init)
    return jnp.einsum('bse,ed->bsd', o.reshape(B, S, H * DV).astype(x.dtype), w_out)


def moe_ffn(x, router_w, router_b, w_gate, w_up, w_down):
    B, S, D = x.shape
    NB = S // MOE_BLOCK
    xb = x.reshape(B, NB, MOE_BLOCK, D).transpose(1, 0, 2, 3).reshape(NB, B * MOE_BLOCK, D)
    expert_group = jnp.arange(N_EXPERTS) // EXPERTS_PER_GROUP

    def block(xt):
        probs = jax.nn.softmax((xt @ router_w).astype(jnp.float32), axis=-1)
        sel = probs + router_b
        gscore = lax.top_k(sel.reshape(-1, N_GROUPS, EXPERTS_PER_GROUP), TOP_K)[0].sum(-1)
        gbest = jnp.argmax(gscore, axis=-1)
        in_group = expert_group[None, :] == gbest[:, None]
        _, idx = lax.top_k(jnp.where(in_group, sel, -jnp.inf), TOP_K)
        gates = jnp.take_along_axis(probs, idx, axis=-1)
        gates = gates / jnp.sum(gates, axis=-1, keepdims=True)
        comb = jnp.sum(jax.nn.one_hot(idx, N_EXPERTS, dtype=jnp.float32) * gates[..., None], axis=1)
        h = (jax.nn.silu(jnp.einsum('td,edf->tef', xt, w_gate))
             * jnp.einsum('td,edf->tef', xt, w_up))
        h = h * comb[:, :, None].astype(h.dtype)
        return jnp.einsum('tef,efd->td', h, w_down)

    y = lax.map(block, xb)
    return y.reshape(NB, B, MOE_BLOCK, D).transpose(1, 0, 2, 3).reshape(B, S, D)


def setup_inputs(seed: int = 0) -> dict:
    key = jax.random.key(seed)
    ks = jax.random.split(key, 20)
    D = D_MODEL
    nrm = jax.random.normal
    f32 = jnp.float32
    d_a = HGRN_HEADS * HGRN_DK
    d_bq = 2 * DIFF_HEADS * DIFF_DH
    d_bv = DIFF_HEADS * DIFF_DV
    return {
        "x": nrm(ks[0], (BATCH, SEQ, D), f32),
        "a_w_in": nrm(ks[1], (N_A_LAYERS, D, 4 * d_a), f32) * D ** -0.5,
        "a_lb": nrm(ks[2], (N_A_LAYERS + 1, d_a), f32) * 0.1,
        "a_norm_g": 1.0 + 0.02 * nrm(ks[3], (N_A_LAYERS, HGRN_HEADS, HGRN_DV), f32),
        "a_w_out": nrm(ks[4], (N_A_LAYERS, d_a, D), f32) * d_a ** -0.5 * DEEPNORM_BETA,
        "kv_w": nrm(ks[5], (D, d_bq + d_bv), f32) * D ** -0.5,
        "b_w_q": nrm(ks[6], (N_B_LAYERS, D, d_bq), f32) * D ** -0.5,
        "b_lam": nrm(ks[7], (N_B_LAYERS, 4, DIFF_DH), f32) * 0.1,
        "b_subln_g": 1.0 + 0.02 * nrm(ks[8], (N_B_LAYERS, DIFF_DV), f32),
        "b_w_out": nrm(ks[9], (N_B_LAYERS, d_bv, D), f32) * d_bv ** -0.5 * DEEPNORM_BETA,
        "ln1_g": 1.0 + 0.02 * nrm(ks[10], (DEPTH, D), f32),
        "ln1_b": 0.02 * nrm(ks[11], (DEPTH, D), f32),
        "ln2_g": 1.0 + 0.02 * nrm(ks[12], (DEPTH, D), f32),
        "ln2_b": 0.02 * nrm(ks[13], (DEPTH, D), f32),
        "router_w": nrm(ks[14], (D, N_EXPERTS), f32) * D ** -0.5,
        "router_b": 0.01 * nrm(ks[15], (N_EXPERTS,), f32),
        "moe_w_gate": nrm(ks[16], (DEPTH, N_EXPERTS, D, D_EXPERT), f32) * D ** -0.5,
        "moe_w_up": nrm(ks[17], (DEPTH, N_EXPERTS, D, D_EXPERT), f32) * D ** -0.5,
        "moe_w_down": nrm(ks[18], (DEPTH, N_EXPERTS, D_EXPERT, D), f32) * D_EXPERT ** -0.5 * DEEPNORM_BETA,
    }


def reference(x, a_w_in, a_lb, a_norm_g, a_w_out, kv_w, b_w_q, b_lam, b_subln_g, b_w_out,
              ln1_g, ln1_b, ln2_g, ln2_b, router_w, router_b, moe_w_gate, moe_w_up, moe_w_down):
    lb_all = jnp.cumsum(jax.nn.softmax(a_lb.astype(jnp.float32), axis=0), axis=0)
    k_sh, v_sh = None, None
    for l in range(DEPTH):
        if l < N_A_LAYERS:
            mix = hgrn2_mixer(x, a_w_in[l], lb_all[l], a_norm_g[l], a_w_out[l])
        else:
            if l == N_A_LAYERS:
                k_sh, v_sh = shared_kv(x, kv_w)
            j = l - N_A_LAYERS
            lambda_init = 0.8 - 0.6 * math.exp(-0.3 * l)
            mix = diff_attention(x, k_sh, v_sh, b_w_q[j], b_lam[j], b_subln_g[j], b_w_out[j],
                                 lambda_init)
        x = layer_norm(DEEPNORM_ALPHA * x + mix, ln1_g[l], ln1_b[l])
        ffn = moe_ffn(x, router_w, router_b, moe_w_gate[l], moe_w_up[l], moe_w_down[l])
        x = layer_norm(DEEPNORM_ALPHA * x + ffn, ln2_g[l], ln2_b[l])
    return x
```

```python
import functools
import math

import jax
import jax.numpy as jnp
from jax import lax
from jax.experimental import pallas as pl
from jax.experimental.pallas import tpu as pltpu

F32 = jnp.float32
BF16 = jnp.bfloat16
I32 = jnp.int32

D_MODEL = 1024
DEPTH = 2
N_A_LAYERS = DEPTH // 2
HGRN_HEADS = 8
HGRN_DH = D_MODEL // HGRN_HEADS
HGRN_CHUNK = 64
DIFF_HEADS = 8
DIFF_DH = D_MODEL // (2 * DIFF_HEADS)
DIFF_DV = 2 * DIFF_DH
N_EXPERTS = 16
N_GROUPS = 4
EXPERTS_PER_GROUP = N_EXPERTS // N_GROUPS
PAIRS_PER_GROUP = 6
N_CLASSES = N_GROUPS * PAIRS_PER_GROUP
D_EXPERT = 512
DEEPNORM_ALPHA = (2.0 * DEPTH) ** 0.25
LN_EPS = 1e-5
RMS_EPS = 1e-6

LANES = 128
INFO_W = LANES
ROW_W = D_MODEL + INFO_W
MOE_TILE = 256
DMA_CHUNK = 512
NEG_BIG = -1e30
VMEM_LIMIT = 56 * 1024 * 1024


def _cparams(*sem):
    return pltpu.CompilerParams(dimension_semantics=sem, vmem_limit_bytes=VMEM_LIMIT)


def _dot(a, b):
    return jnp.dot(a, b, preferred_element_type=F32)


def _dot_nt(a, b):
    return lax.dot_general(a, b, (((1,), (1,)), ((), ())), preferred_element_type=F32)


def _dot_tn(a, b):
    return lax.dot_general(a, b, (((0,), (0,)), ((), ())), preferred_element_type=F32)


def _split_bf16(x):
    hi = x.astype(BF16)
    lo = (x - hi.astype(F32)).astype(BF16)
    return hi, lo


def _hgrn_in_kernel(x_ref, w_ref, lb_ref, q_ref, k_ref, v_ref, g_ref, lf_ref, *, layer):
    d = D_MODEL
    x = x_ref[...].astype(BF16)
    a = lb_ref[...]
    e = jnp.exp(a - jnp.max(a, axis=0, keepdims=True))
    sm = e / jnp.sum(e, axis=0, keepdims=True)
    lb = jnp.sum(sm[0:layer + 1], axis=0, keepdims=True)
    q_ref[...] = _dot(x, w_ref[:, 0:d]).astype(BF16)
    f = lb + (1.0 - lb) * jax.nn.sigmoid(_dot(x, w_ref[:, d:2 * d]))
    lf_ref[...] = jnp.log(f)
    k_ref[...] = (1.0 - f).astype(BF16)
    v_ref[...] = _dot(x, w_ref[:, 2 * d:3 * d]).astype(BF16)
    g_ref[...] = jax.nn.sigmoid(_dot(x, w_ref[:, 3 * d:4 * d])).astype(BF16)


def _hgrn_in(x, w_in_bf16, a_lb, layer, tm):
    t, d = x.shape
    row = pl.BlockSpec((tm, d), lambda i: (i, 0))
    return pl.pallas_call(
        functools.partial(_hgrn_in_kernel, layer=layer),
        grid=(t // tm,),
        in_specs=[row,
                  pl.BlockSpec((d, 4 * d), lambda i: (0, 0)),
                  pl.BlockSpec(a_lb.shape, lambda i: (0, 0))],
        out_specs=[row, row, row, row, row],
        out_shape=[jax.ShapeDtypeStruct((t, d), BF16)] * 4 + [jax.ShapeDtypeStruct((t, d), F32)],
        compiler_params=_cparams("parallel"),
        name="hgrn_in",
    )(x, w_in_bf16, a_lb)


def _hgrn_scan_kernel(q_ref, k_ref, v_ref, lf_ref, g_ref, ng_ref, o_ref, state_ref, *, n_chunks):
    c_len, dh = HGRN_CHUNK, HGRN_DH

    @pl.when(pl.program_id(1) == 0)
    def _():
        state_ref[...] = jnp.zeros_like(state_ref)

    row = lax.broadcasted_iota(I32, (c_len, c_len), 0)
    col = lax.broadcasted_iota(I32, (c_len, c_len), 1)
    causal = col <= row
    tri = jnp.where(causal, 1.0, 0.0).astype(BF16)

    def chunk(c, carry):
        r0 = pl.multiple_of(c * c_len, c_len)
        rows = pl.ds(r0, c_len)
        for h in range(HGRN_HEADS):
            cs = slice(h * dh, (h + 1) * dh)
            lf_hi, lf_lo = _split_bf16(lf_ref[rows, cs])
            b = _dot(tri, lf_hi) + _dot(tri, lf_lo)
            q = q_ref[rows, cs].astype(F32)
            k = k_ref[rows, cs].astype(F32)
            v = v_ref[rows, cs]
            b_mid = b[c_len // 2 - 1:c_len // 2, :]
            b_last = b[c_len - 1:c_len, :]
            qs = (q * jnp.exp(b - b_mid)).astype(BF16)
            ks = (k * jnp.exp(b_mid - b)).astype(BF16)
            s = jnp.where(causal, _dot_nt(qs, ks), 0.0)
            st = state_ref[h]
            o = _dot(s.astype(BF16), v) + _dot_nt((q * jnp.exp(b)).astype(BF16), st.astype(BF16))
            kd = (k * jnp.exp(b_last - b)).astype(BF16)
            state_ref[h] = jnp.exp(b_last) * st + _dot_tn(v, kd)
            o = o * lax.rsqrt(jnp.mean(o * o, axis=-1, keepdims=True) + RMS_EPS) * ng_ref[h:h + 1, :]
            o_ref[rows, cs] = (o * g_ref[rows, cs].astype(F32)).astype(BF16)
        return carry

    lax.fori_loop(0, n_chunks, chunk, 0)


def _hgrn_scan(q, k, v, lf, g, norm_g, batch, seq, rows_per_step):
    t, d = q.shape
    steps = seq // rows_per_step
    blk = pl.BlockSpec((rows_per_step, d), lambda b, s: (b * steps + s, 0))
    return pl.pallas_call(
        functools.partial(_hgrn_scan_kernel, n_chunks=rows_per_step // HGRN_CHUNK),
        grid=(batch, steps),
        in_specs=[blk, blk, blk, blk, blk, pl.BlockSpec(norm_g.shape, lambda b, s: (0, 0))],
        out_specs=blk,
        out_shape=jax.ShapeDtypeStruct((t, d), BF16),
        scratch_shapes=[pltpu.VMEM((HGRN_HEADS, HGRN_DH, HGRN_DH), F32)],
        compiler_params=_cparams("parallel", "arbitrary"),
        name="hgrn_scan",
    )(q, k, v, lf, g, norm_g)


def _route(logits_t, rb):
    e = jnp.exp(logits_t - jnp.max(logits_t, axis=0, keepdims=True))
    probs = e / jnp.sum(e, axis=0, keepdims=True)
    sel = probs + rb
    selr = [sel[i:i + 1, :] for i in range(N_EXPERTS)]
    pr = [probs[i:i + 1, :] for i in range(N_EXPERTS)]
    n_per = EXPERTS_PER_GROUP

    def top2_sum(vals):
        best = None
        for i in range(len(vals)):
            for j in range(i + 1, len(vals)):
                pair = vals[i] + vals[j]
                best = pair if best is None else jnp.maximum(best, pair)
        return best

    gscore = [top2_sum(selr[g * n_per:(g + 1) * n_per]) for g in range(N_GROUPS)]
    gbest = jnp.zeros_like(gscore[0], dtype=I32)
    gval = gscore[0]
    for g in range(1, N_GROUPS):
        better = gscore[g] > gval
        gbest = jnp.where(better, g, gbest)
        gval = jnp.where(better, gscore[g], gval)

    def in_best_group(rows, i):
        out = rows[(N_GROUPS - 1) * n_per + i]
        for g in range(N_GROUPS - 2, -1, -1):
            out = jnp.where(gbest == g, rows[g * n_per + i], out)
        return out

    sv = [in_best_group(selr, i) for i in range(n_per)]
    pv = [in_best_group(pr, i) for i in range(n_per)]
    i1 = jnp.zeros_like(gbest)
    v1, p1 = sv[0], pv[0]
    for i in range(1, n_per):
        better = sv[i] > v1
        i1 = jnp.where(better, i, i1)
        v1 = jnp.where(better, sv[i], v1)
        p1 = jnp.where(better, pv[i], p1)
    i2 = jnp.full_like(gbest, -1)
    v2 = jnp.full_like(v1, -jnp.inf)
    p2 = jnp.zeros_like(p1)
    for i in range(n_per):
        better = (i1 != i) & (sv[i] > v2)
        i2 = jnp.where(better, i, i2)
        v2 = jnp.where(better, sv[i], v2)
        p2 = jnp.where(better, pv[i], p2)
    den = p1 + p2
    g1, g2 = p1 / den, p2 / den
    first_lower = i1 < i2
    lo = jnp.where(first_lower, i1, i2)
    hi = jnp.where(first_lower, i2, i1)
    pair = jnp.where(lo == 0, hi - 1, jnp.where(lo == 1, hi + 1, PAIRS_PER_GROUP - 1))
    cls = gbest * PAIRS_PER_GROUP + pair
    return cls, jnp.where(first_lower, g1, g2), jnp.where(first_lower, g2, g1)


def _post_mix_kernel(x_ref, o_ref, w_ref, g_ref, b_ref, rwh_ref, rwl_ref, rb_ref, out_ref):
    d = D_MODEL
    z = DEEPNORM_ALPHA * x_ref[...] + _dot(o_ref[...], w_ref[...])
    mu = jnp.mean(z, axis=-1, keepdims=True)
    zc = z - mu
    var = jnp.mean(zc * zc, axis=-1, keepdims=True)
    y = zc * lax.rsqrt(var + LN_EPS) * g_ref[...] + b_ref[...]
    out_ref[:, 0:d] = y
    yh, yl = _split_bf16(y)
    logits = _dot(yh, rwh_ref[...]) + _dot(yl, rwh_ref[...]) + _dot(yh, rwl_ref[...])
    cls, g_lo, g_hi = _route(jnp.transpose(logits)[0:N_EXPERTS, :], rb_ref[...])
    n = cls.shape[1]
    r = lax.broadcasted_iota(I32, (8, n), 0)
    info = jnp.where(r == 0, cls.astype(F32), jnp.where(r == 1, g_lo, jnp.where(r == 2, g_hi, 0.0)))
    info = jnp.concatenate([info, jnp.zeros((INFO_W - 8, n), F32)], axis=0)
    out_ref[:, d:d + INFO_W] = jnp.transpose(info)


def _post_mix(x, o, w_out_bf16, ln_g, ln_b, rw_hi, rw_lo, rb_col, tm):
    t, d = o.shape
    vec = pl.BlockSpec((1, d), lambda i: (0, 0))
    return pl.pallas_call(
        _post_mix_kernel,
        grid=(t // tm,),
        in_specs=[pl.BlockSpec((tm, d), lambda i: (i, 0)),
                  pl.BlockSpec((tm, d), lambda i: (i, 0)),
                  pl.BlockSpec((d, d), lambda i: (0, 0)),
                  vec, vec,
                  pl.BlockSpec((d, LANES), lambda i: (0, 0)),
                  pl.BlockSpec((d, LANES), lambda i: (0, 0)),
                  pl.BlockSpec((N_EXPERTS, 1), lambda i: (0, 0))],
        out_specs=pl.BlockSpec((tm, ROW_W), lambda i: (i, 0)),
        out_shape=jax.ShapeDtypeStruct((t, ROW_W), F32),
        compiler_params=_cparams("parallel"),
        name="post_mix",
    )(x, o, w_out_bf16, ln_g, ln_b, rw_hi, rw_lo, rb_col)


def _plan_kernel(info_ref, pos_ref, cnt_ref, cnt_s, base_s, run_s, *, tile):
    phase, i = pl.program_id(0), pl.program_id(1)
    tb = info_ref.shape[0]
    cls = info_ref[:, 0:1]
    lane = lax.broadcasted_iota(I32, (tb, LANES), 1).astype(F32)
    onehot = cls == lane
    ohf = jnp.where(onehot, 1.0, 0.0)
    block_cnt = jnp.sum(ohf, axis=0, keepdims=True)

    @pl.when((phase == 0) & (i == 0))
    def _():
        cnt_s[...] = jnp.zeros_like(cnt_s)

    @pl.when(phase == 0)
    def _():
        cnt_s[...] += block_cnt

    @pl.when((phase == 1) & (i == 0))
    def _():
        cnt = cnt_s[...]
        ntile = jnp.floor((cnt + (tile - 1)) * (1.0 / tile))
        nt_hi = jnp.floor(ntile * (1.0 / 256.0))
        nt_lo = ntile - 256.0 * nt_hi
        r = lax.broadcasted_iota(I32, (LANES, LANES), 0)
        c = lax.broadcasted_iota(I32, (LANES, LANES), 1)
        upper = jnp.where(r < c, 1.0, 0.0).astype(BF16)
        excl = (256.0 * _dot(jnp.broadcast_to(nt_hi, (8, LANES)).astype(BF16), upper)
                + _dot(jnp.broadcast_to(nt_lo, (8, LANES)).astype(BF16), upper))
        base_s[...] = excl[0:1, :] * tile
        run_s[...] = jnp.zeros_like(run_s)
        cnt_ref[...] = cnt

    @pl.when(phase == 1)
    def _():
        r = lax.broadcasted_iota(I32, (tb, tb), 0)
        c = lax.broadcasted_iota(I32, (tb, tb), 1)
        lower = jnp.where(c <= r, 1.0, 0.0).astype(BF16)
        within = _dot(lower, ohf.astype(BF16))
        rank = within + (run_s[...] + base_s[...] - 1.0)
        pos_col = jnp.sum(jnp.where(onehot, rank, 0.0), axis=1, keepdims=True)
        pos_row = jnp.transpose(jnp.broadcast_to(pos_col, (tb, LANES)))[0:1, :]
        pos_ref[...] = pos_row.astype(I32)
        run_s[...] += block_cnt


def _plan(xg, tb, tile):
    t = xg.shape[0]
    info_col = D_MODEL // INFO_W
    return pl.pallas_call(
        functools.partial(_plan_kernel, tile=tile),
        grid=(2, t // tb),
        in_specs=[pl.BlockSpec((tb, INFO_W), lambda p, i: (i, info_col))],
        out_specs=[pl.BlockSpec((1, tb), lambda p, i: (0, i * p)),
                   pl.BlockSpec((1, LANES), lambda p, i: (0, 0))],
        out_shape=[jax.ShapeDtypeStruct((1, t), I32), jax.ShapeDtypeStruct((1, LANES), F32)],
        scratch_shapes=[pltpu.VMEM((1, LANES), F32)] * 3,
        compiler_params=_cparams("arbitrary", "arbitrary"),
        name="moe_plan",
    )(xg)


def _row_copy(src, si, dst, di, sem):
    return pltpu.make_async_copy(src.at[pl.ds(si, 1)], dst.at[pl.ds(di, 1)], sem)


def _scatter_rows_kernel(pos_ref, pad_lo_ref, pad_hi_ref, nvalid_ref, x_hbm, xs_hbm,
                         zrow, ztile, sems, zsem, *, tile):
    t = x_hbm.shape[0]
    n_chunks = t // DMA_CHUNK
    n_tiles = xs_hbm.shape[0] // tile

    ztile[...] = jnp.zeros_like(ztile)

    def tail_copy(k):
        return pltpu.make_async_copy(ztile, xs_hbm.at[pl.ds(k * tile, tile)], zsem.at[1])

    def tail_start(k, carry):
        tail_copy(k).start()
        return carry

    def tail_wait(k, carry):
        tail_copy(k).wait()
        return carry

    lax.fori_loop(nvalid_ref[0], n_tiles, tail_start, 0)

    def issue(c):
        def body(j, carry):
            tok = c * DMA_CHUNK + j
            _row_copy(x_hbm, tok, xs_hbm, pos_ref[tok], sems.at[c % 2]).start()
            return carry
        lax.fori_loop(0, DMA_CHUNK, body, 0)

    def drain(c):
        def body(j, carry):
            _row_copy(x_hbm, 0, xs_hbm, 0, sems.at[c % 2]).wait()
            return carry
        lax.fori_loop(0, DMA_CHUNK, body, 0)

    issue(0)

    def step(c, carry):
        issue(c)
        drain(c - 1)
        return carry

    lax.fori_loop(1, n_chunks, step, 0)
    drain(n_chunks - 1)

    zrow[...] = jnp.zeros_like(zrow)
    for cls in range(N_CLASSES):
        lo, hi = pad_lo_ref[cls], pad_hi_ref[cls]

        def fill(r, carry):
            pltpu.make_async_copy(zrow, xs_hbm.at[pl.ds(r, 1)], zsem.at[0]).start()
            return carry

        def fill_wait(r, carry):
            pltpu.make_async_copy(zrow, xs_hbm.at[pl.ds(r, 1)], zsem.at[0]).wait()
            return carry

        lax.fori_loop(lo, hi, fill, 0)
        lax.fori_loop(lo, hi, fill_wait, 0)

    lax.fori_loop(nvalid_ref[0], n_tiles, tail_wait, 0)


def _scatter_rows(pos, pad_lo, pad_hi, nvalid, xg, n_sorted, tile):
    t, w = xg.shape
    any_spec = pl.BlockSpec(memory_space=pl.ANY)
    return pl.pallas_call(
        functools.partial(_scatter_rows_kernel, tile=tile),
        grid_spec=pltpu.PrefetchScalarGridSpec(
            num_scalar_prefetch=4, grid=(1,),
            in_specs=[any_spec], out_specs=any_spec,
            scratch_shapes=[pltpu.VMEM((1, w), F32),
                            pltpu.VMEM((tile, w), F32),
                            pltpu.SemaphoreType.DMA((2,)),
                            pltpu.SemaphoreType.DMA((2,))]),
        out_shape=jax.ShapeDtypeStruct((n_sorted, w), F32),
        compiler_params=_cparams("arbitrary"),
        name="moe_scatter_rows",
    )(pos, pad_lo, pad_hi, nvalid, xg)


def _gather_rows_kernel(pos_ref, ys_hbm, y_hbm, sems):
    t = y_hbm.shape[0]
    n_chunks = t // DMA_CHUNK

    def issue(c):
        def body(j, carry):
            tok = c * DMA_CHUNK + j
            _row_copy(ys_hbm, pos_ref[tok], y_hbm, tok, sems.at[c % 2]).start()
            return carry
        lax.fori_loop(0, DMA_CHUNK, body, 0)

    def drain(c):
        def body(j, carry):
            _row_copy(ys_hbm, 0, y_hbm, 0, sems.at[c % 2]).wait()
            return carry
        lax.fori_loop(0, DMA_CHUNK, body, 0)

    issue(0)

    def step(c, carry):
        issue(c)
        drain(c - 1)
        return carry

    lax.fori_loop(1, n_chunks, step, 0)
    drain(n_chunks - 1)


def _gather_rows(pos, ys, t):
    any_spec = pl.BlockSpec(memory_space=pl.ANY)
    return pl.pallas_call(
        _gather_rows_kernel,
        grid_spec=pltpu.PrefetchScalarGridSpec(
            num_scalar_prefetch=1, grid=(1,),
            in_specs=[any_spec], out_specs=any_spec,
            scratch_shapes=[pltpu.SemaphoreType.DMA((2,))]),
        out_shape=jax.ShapeDtypeStruct((t, ys.shape[1]), F32),
        compiler_params=_cparams("arbitrary"),
        name="moe_gather_rows",
    )(pos, ys)


def _experts_kernel(ea_ref, eb_ref, src_ref, nvalid_ref, xs_ref,
                    wga_ref, wua_ref, wda_ref, wgb_ref, wub_ref, wdb_ref, ys_ref):
    d = D_MODEL

    @pl.when(pl.program_id(0) < nvalid_ref[0])
    def _():
        x = xs_ref[:, 0:d].astype(BF16)
        g_lo = xs_ref[:, d + 1:d + 2]
        g_hi = xs_ref[:, d + 2:d + 3]
        ha = jax.nn.silu(_dot(x, wga_ref[0])) * _dot(x, wua_ref[0]) * g_lo
        hb = jax.nn.silu(_dot(x, wgb_ref[0])) * _dot(x, wub_ref[0]) * g_hi
        ys_ref[...] = _dot(ha.astype(BF16), wda_ref[0]) + _dot(hb.astype(BF16), wdb_ref[0])

    @pl.when(pl.program_id(0) >= nvalid_ref[0])
    def _():
        ys_ref[...] = jnp.zeros_like(ys_ref)


def _experts(ea, eb, src, nvalid, xs, w_gate, w_up, w_down, tile):
    n_sorted = xs.shape[0]
    d, f = D_MODEL, D_EXPERT
    up_a = pl.BlockSpec((1, d, f), lambda i, ea, eb, src, nv: (ea[i], 0, 0))
    up_b = pl.BlockSpec((1, d, f), lambda i, ea, eb, src, nv: (eb[i], 0, 0))
    dn_a = pl.BlockSpec((1, f, d), lambda i, ea, eb, src, nv: (ea[i], 0, 0))
    dn_b = pl.BlockSpec((1, f, d), lambda i, ea, eb, src, nv: (eb[i], 0, 0))
    return pl.pallas_call(
        _experts_kernel,
        grid_spec=pltpu.PrefetchScalarGridSpec(
            num_scalar_prefetch=4, grid=(n_sorted // tile,),
            in_specs=[pl.BlockSpec((tile, ROW_W), lambda i, ea, eb, src, nv: (src[i], 0)),
                      up_a, up_a, dn_a, up_b, up_b, dn_b],
            out_specs=pl.BlockSpec((tile, d), lambda i, ea, eb, src, nv: (i, 0))),
        out_shape=jax.ShapeDtypeStruct((n_sorted, d), F32),
        compiler_params=_cparams("arbitrary"),
        name="moe_experts",
    )(ea, eb, src, nvalid, xs, w_gate, w_up, w_down, w_gate, w_up, w_down)


def _ln2_kernel(x_ref, y_ref, g_ref, b_ref, o_ref):
    z = DEEPNORM_ALPHA * x_ref[...] + y_ref[...]
    mu = jnp.mean(z, axis=-1, keepdims=True)
    zc = z - mu
    var = jnp.mean(zc * zc, axis=-1, keepdims=True)
    o_ref[...] = zc * lax.rsqrt(var + LN_EPS) * g_ref[...] + b_ref[...]


def _ln2(xg, y, ln_g, ln_b, tm):
    t, d = y.shape
    row = pl.BlockSpec((tm, d), lambda i: (i, 0))
    vec = pl.BlockSpec((1, d), lambda i: (0, 0))
    return pl.pallas_call(
        _ln2_kernel,
        grid=(t // tm,),
        in_specs=[row, row, vec, vec],
        out_specs=row,
        out_shape=jax.ShapeDtypeStruct((t, d), F32),
        compiler_params=_cparams("parallel"),
        name="ln2",
    )(xg, y, ln_g, ln_b)


def _moe_layer(xg, w_gate, w_up, w_down, ln_g, ln_b, tm):
    t = xg.shape[0]
    tile = min(MOE_TILE, t)
    n_tiles = t // tile + N_CLASSES
    pos, cnt = _plan(xg, min(512, t), tile)
    pos = pos.reshape(t)
    cnt = cnt[0, :N_CLASSES].astype(I32)
    ntile = (cnt + tile - 1) // tile
    tile_end = jnp.cumsum(ntile)
    base = (tile_end - ntile) * tile
    nvalid = tile_end[-1]
    tid = jnp.minimum(jnp.arange(n_tiles, dtype=I32), nvalid - 1)
    tcls = jnp.sum((tid[:, None] >= tile_end[None, :]).astype(I32), axis=1)
    grp, pair = tcls // PAIRS_PER_GROUP, tcls % PAIRS_PER_GROUP
    pair_lo = jnp.array([0, 0, 0, 1, 1, 2], I32)[pair]
    pair_hi = jnp.array([1, 2, 3, 2, 3, 3], I32)[pair]
    ea = grp * EXPERTS_PER_GROUP + pair_lo
    eb = grp * EXPERTS_PER_GROUP + pair_hi
    nvalid = nvalid.reshape(1)
    xs = _scatter_rows(pos, base + cnt, base + ntile * tile, nvalid, xg, n_tiles * tile, tile)
    ys = _experts(ea, eb, tid, nvalid, xs, w_gate, w_up, w_down, tile)
    y = _gather_rows(pos, ys, t)
    return _ln2(xg, y, ln_g, ln_b, tm)


def _qkv_kernel(x_ref, w_ref, q_ref, k_ref, v_ref):
    d = D_MODEL
    x = x_ref[...].astype(BF16)
    q_ref[...] = (_dot(x, w_ref[:, 0:d]) * (DIFF_DH ** -0.5)).astype(BF16)
    k_ref[...] = _dot(x, w_ref[:, d:2 * d]).astype(BF16)
    v_ref[...] = _dot(x, w_ref[:, 2 * d:3 * d]).astype(BF16)


def _qkv(x, w_qkv_bf16, tm):
    t, d = x.shape
    row = pl.BlockSpec((tm, d), lambda i: (i, 0))
    return pl.pallas_call(
        _qkv_kernel,
        grid=(t // tm,),
        in_specs=[row, pl.BlockSpec((d, 3 * d), lambda i: (0, 0))],
        out_specs=[row, row, row],
        out_shape=[jax.ShapeDtypeStruct((t, d), BF16)] * 3,
        compiler_params=_cparams("parallel"),
        name="qkv_proj",
    )(x, w_qkv_bf16)


def _diff_attn_kernel(q_ref, k_ref, v_ref, lam_ref, sg_ref, o_ref, m_s, l_s, acc_s, *, tq, lambda_init):
    qi = pl.program_id(2)
    dh = DIFF_DH
    q = q_ref[...]
    lane = lax.broadcasted_iota(I32, q.shape, 1)
    zero = jnp.zeros_like(q)
    qs = jnp.concatenate([jnp.where(lane < dh, q, zero), jnp.where(lane >= dh, q, zero)], axis=0)
    m_s[...] = jnp.full_like(m_s, NEG_BIG)
    l_s[...] = jnp.zeros_like(l_s)
    acc_s[...] = jnp.zeros_like(acc_s)

    def block(j, masked):
        r0 = pl.multiple_of(j * tq, tq)
        s = _dot_nt(qs, k_ref[pl.ds(r0, tq), :])
        if masked:
            row = lax.broadcasted_iota(I32, (tq, tq), 0)
            col = lax.broadcasted_iota(I32, (tq, tq), 1)
            keep = jnp.concatenate([col <= row, col <= row], axis=0)
            s = jnp.where(keep, s, NEG_BIG)
        m_old = m_s[...]
        m_new = jnp.maximum(m_old, jnp.max(s, axis=-1, keepdims=True))
        p = jnp.exp(s - m_new)
        scale = jnp.exp(m_old - m_new)
        l_s[...] = scale * l_s[...] + jnp.sum(p, axis=-1, keepdims=True)
        acc_s[...] = scale * acc_s[...] + _dot(p.astype(BF16), v_ref[pl.ds(r0, tq), :])
        m_s[...] = m_new

    def body(j, carry):
        block(j, False)
        return carry

    lax.fori_loop(0, qi, body, 0)
    block(qi, True)

    lam = lam_ref[...]
    lam_val = (jnp.exp(jnp.sum(lam[0:1] * lam[1:2], axis=1, keepdims=True))
               - jnp.exp(jnp.sum(lam[2:3] * lam[3:4], axis=1, keepdims=True)) + lambda_init)
    o = acc_s[0:tq] / l_s[0:tq] - lam_val * (acc_s[tq:2 * tq] / l_s[tq:2 * tq])
    o = o * lax.rsqrt(jnp.mean(o * o, axis=-1, keepdims=True) + RMS_EPS) * sg_ref[...]
    o_ref[...] = (o * (1.0 - lambda_init)).astype(BF16)


def _diff_attn(q, k, v, lam, subln_g, batch, seq, tq, lambda_init):
    t, d = q.shape
    nq = seq // tq
    dv = DIFF_DV
    return pl.pallas_call(
        functools.partial(_diff_attn_kernel, tq=tq, lambda_init=lambda_init),
        grid=(batch, DIFF_HEADS, nq),
        in_specs=[pl.BlockSpec((tq, dv), lambda b, h, i: (b * nq + i, h)),
                  pl.BlockSpec((seq, dv), lambda b, h, i: (b, h)),
                  pl.BlockSpec((seq, dv), lambda b, h, i: (b, h)),
                  pl.BlockSpec(lam.shape, lambda b, h, i: (0, 0)),
                  pl.BlockSpec((1, dv), lambda b, h, i: (0, 0))],
        out_specs=pl.BlockSpec((tq, dv), lambda b, h, i: (b * nq + i, h)),
        out_shape=jax.ShapeDtypeStruct((t, d), BF16),
        scratch_shapes=[pltpu.VMEM((2 * tq, 1), F32), pltpu.VMEM((2 * tq, 1), F32),
                        pltpu.VMEM((2 * tq, dv), F32)],
        compiler_params=_cparams("parallel", "parallel", "arbitrary"),
        name="diff_attn",
    )(q, k, v, lam, subln_g)


def kernel(x, a_w_in, a_lb, a_norm_g, a_w_out, kv_w, b_w_q, b_lam, b_subln_g, b_w_out,
           ln1_g, ln1_b, ln2_g, ln2_b, router_w, router_b, moe_w_gate, moe_w_up, moe_w_down):
    batch, seq, d = x.shape
    t = batch * seq
    tm = min(512, t)
    xt = x.reshape(t, d)

    rw = jnp.pad(router_w, ((0, 0), (0, LANES - N_EXPERTS)))
    rw_hi = rw.astype(BF16)
    rw_lo = (rw - rw_hi.astype(F32)).astype(BF16)
    rb_col = router_b.reshape(N_EXPERTS, 1)

    for layer in range(DEPTH):
        if layer < N_A_LAYERS:
            q, k, v, g, lf = _hgrn_in(xt, a_w_in[layer].astype(BF16), a_lb, layer, tm)
            o = _hgrn_scan(q, k, v, lf, g, a_norm_g[layer], batch, seq, min(512, seq))
            w_out = a_w_out[layer]
        else:
            j = layer - N_A_LAYERS
            w_qkv = jnp.concatenate([b_w_q[j], kv_w], axis=1).astype(BF16)
            q, k, v = _qkv(xt, w_qkv, tm)
            lambda_init = 0.8 - 0.6 * math.exp(-0.3 * layer)
            o = _diff_attn(q, k, v, b_lam[j], b_subln_g[j].reshape(1, DIFF_DV), batch, seq,
                           min(256, seq), lambda_init)
            w_out = b_w_out[j]
        xg = _post_mix(xt, o, w_out.astype(BF16), ln1_g[layer].reshape(1, d),
                       ln1_b[layer].reshape(1, d), rw_hi, rw_lo, rb_col, tm)
        xt = _moe_layer(xg, moe_w_gate[layer].astype(BF16), moe_w_up[layer].astype(BF16),
                        moe_w_down[layer].astype(BF16), ln2_g[layer].reshape(1, d),
                        ln2_b[layer].reshape(1, d), tm)
    return xt.reshape(batch, seq, d)
```

```python
import functools
import math

import jax
import jax.numpy as jnp
from jax import lax
from jax.experimental import pallas as pl
from jax.experimental.pallas import tpu as pltpu

F32 = jnp.float32
BF16 = jnp.bfloat16
I32 = jnp.int32

D_MODEL = 1024
DEPTH = 2
N_A_LAYERS = DEPTH // 2
HGRN_HEADS = 8
HGRN_DH = D_MODEL // HGRN_HEADS
HGRN_CHUNK = 64
DIFF_HEADS = 8
DIFF_DH = D_MODEL // (2 * DIFF_HEADS)
DIFF_DV = 2 * DIFF_DH
N_EXPERTS = 16
N_GROUPS = 4
EXPERTS_PER_GROUP = N_EXPERTS // N_GROUPS
PAIRS_PER_GROUP = 6
N_CLASSES = N_GROUPS * PAIRS_PER_GROUP
CLASS_ROWS = 32
D_EXPERT = 512
DEEPNORM_ALPHA = (2.0 * DEPTH) ** 0.25
LN_EPS = 1e-5
RMS_EPS = 1e-6

LANES = 128
MOE_TILE = 256
LN2_TILE = 256
ATTN_TQ = 256
ATTN_HEADS_PER_STEP = 2
NEG_BIG = -1e30
VMEM_LIMIT = 56 * 1024 * 1024


def _cparams(*sem):
    return pltpu.CompilerParams(dimension_semantics=sem, vmem_limit_bytes=VMEM_LIMIT)


def _dot(a, b):
    return jnp.dot(a, b, preferred_element_type=F32)


def _dot_nt(a, b):
    return lax.dot_general(a, b, (((1,), (1,)), ((), ())), preferred_element_type=F32)


def _dot_tn(a, b):
    return lax.dot_general(a, b, (((0,), (0,)), ((), ())), preferred_element_type=F32)


def _split_bf16(x):
    hi = x.astype(BF16)
    lo = (x - hi.astype(F32)).astype(BF16)
    return hi, lo


def _layer_norm(z, g, b):
    mu = jnp.mean(z, axis=-1, keepdims=True)
    zc = z - mu
    var = jnp.mean(zc * zc, axis=-1, keepdims=True)
    return zc * lax.rsqrt(var + LN_EPS) * g + b


def _hgrn_in_kernel(x_ref, w_ref, lb_ref, q_ref, k_ref, v_ref, g_ref, lf_ref, *, layer):
    d = D_MODEL
    x = x_ref[...].astype(BF16)
    a = lb_ref[...]
    e = jnp.exp(a - jnp.max(a, axis=0, keepdims=True))
    sm = e / jnp.sum(e, axis=0, keepdims=True)
    lb = jnp.sum(sm[0:layer + 1], axis=0, keepdims=True)
    q_ref[...] = _dot(x, w_ref[:, 0:d]).astype(BF16)
    f = lb + (1.0 - lb) * jax.nn.sigmoid(_dot(x, w_ref[:, d:2 * d]))
    lf_ref[...] = jnp.log(f)
    k_ref[...] = (1.0 - f).astype(BF16)
    v_ref[...] = _dot(x, w_ref[:, 2 * d:3 * d]).astype(BF16)
    g_ref[...] = jax.nn.sigmoid(_dot(x, w_ref[:, 3 * d:4 * d])).astype(BF16)


def _hgrn_in(x, w_in_bf16, a_lb, layer, tm):
    t, d = x.shape
    row = pl.BlockSpec((tm, d), lambda i: (i, 0))
    return pl.pallas_call(
        functools.partial(_hgrn_in_kernel, layer=layer),
        grid=(t // tm,),
        in_specs=[row,
                  pl.BlockSpec((d, 4 * d), lambda i: (0, 0)),
                  pl.BlockSpec(a_lb.shape, lambda i: (0, 0))],
        out_specs=[row, row, row, row, row],
        out_shape=[jax.ShapeDtypeStruct((t, d), BF16)] * 4 + [jax.ShapeDtypeStruct((t, d), F32)],
        compiler_params=_cparams("parallel"),
        name="hgrn_in",
    )(x, w_in_bf16, a_lb)


def _hgrn_scan_kernel(q_ref, k_ref, v_ref, lf_ref, g_ref, ng_ref, o_ref, state_ref, *, n_chunks):
    c_len, dh = HGRN_CHUNK, HGRN_DH

    @pl.when(pl.program_id(1) == 0)
    def _():
        state_ref[...] = jnp.zeros_like(state_ref)

    row = lax.broadcasted_iota(I32, (c_len, c_len), 0)
    col = lax.broadcasted_iota(I32, (c_len, c_len), 1)
    causal = col <= row
    tri = jnp.where(causal, 1.0, 0.0).astype(BF16)

    def chunk(c, carry):
        r0 = pl.multiple_of(c * c_len, c_len)
        rows = pl.ds(r0, c_len)
        for h in range(HGRN_HEADS):
            cs = slice(h * dh, (h + 1) * dh)
            lf_hi, lf_lo = _split_bf16(lf_ref[rows, cs])
            b = _dot(tri, lf_hi) + _dot(tri, lf_lo)
            q = q_ref[rows, cs].astype(F32)
            k = k_ref[rows, cs].astype(F32)
            v = v_ref[rows, cs]
            b_mid = b[c_len // 2 - 1:c_len // 2, :]
            b_last = b[c_len - 1:c_len, :]
            qs = (q * jnp.exp(b - b_mid)).astype(BF16)
            ks = (k * jnp.exp(b_mid - b)).astype(BF16)
            s = jnp.where(causal, _dot_nt(qs, ks), 0.0)
            st = state_ref[h]
            o = _dot(s.astype(BF16), v) + _dot_nt((q * jnp.exp(b)).astype(BF16), st.astype(BF16))
            kd = (k * jnp.exp(b_last - b)).astype(BF16)
            state_ref[h] = jnp.exp(b_last) * st + _dot_tn(v, kd)
            o = o * lax.rsqrt(jnp.mean(o * o, axis=-1, keepdims=True) + RMS_EPS) * ng_ref[h:h + 1, :]
            o_ref[rows, cs] = (o * g_ref[rows, cs].astype(F32)).astype(BF16)
        return carry

    lax.fori_loop(0, n_chunks, chunk, 0)


def _hgrn_scan(q, k, v, lf, g, norm_g, batch, seq, rows_per_step):
    t, d = q.shape
    steps = seq // rows_per_step
    blk = pl.BlockSpec((rows_per_step, d), lambda b, s: (b * steps + s, 0))
    return pl.pallas_call(
        functools.partial(_hgrn_scan_kernel, n_chunks=rows_per_step // HGRN_CHUNK),
        grid=(batch, steps),
        in_specs=[blk, blk, blk, blk, blk, pl.BlockSpec(norm_g.shape, lambda b, s: (0, 0))],
        out_specs=blk,
        out_shape=jax.ShapeDtypeStruct((t, d), BF16),
        scratch_shapes=[pltpu.VMEM((HGRN_HEADS, HGRN_DH, HGRN_DH), F32)],
        compiler_params=_cparams("parallel", "arbitrary"),
        name="hgrn_scan",
    )(q, k, v, lf, g, norm_g)


def _route_class(logits_t, rb):
    e = jnp.exp(logits_t - jnp.max(logits_t, axis=0, keepdims=True))
    sel = e / jnp.sum(e, axis=0, keepdims=True) + rb
    selr = [sel[i:i + 1, :] for i in range(N_EXPERTS)]
    n_per = EXPERTS_PER_GROUP

    def top2_sum(vals):
        best = None
        for i in range(len(vals)):
            for j in range(i + 1, len(vals)):
                pair = vals[i] + vals[j]
                best = pair if best is None else jnp.maximum(best, pair)
        return best

    gscore = [top2_sum(selr[g * n_per:(g + 1) * n_per]) for g in range(N_GROUPS)]
    gbest = jnp.zeros_like(gscore[0], dtype=I32)
    gval = gscore[0]
    for g in range(1, N_GROUPS):
        better = gscore[g] > gval
        gbest = jnp.where(better, g, gbest)
        gval = jnp.where(better, gscore[g], gval)

    def in_best_group(i):
        out = selr[(N_GROUPS - 1) * n_per + i]
        for g in range(N_GROUPS - 2, -1, -1):
            out = jnp.where(gbest == g, selr[g * n_per + i], out)
        return out

    sv = [in_best_group(i) for i in range(n_per)]
    i1 = jnp.zeros_like(gbest)
    v1 = sv[0]
    for i in range(1, n_per):
        better = sv[i] > v1
        i1 = jnp.where(better, i, i1)
        v1 = jnp.where(better, sv[i], v1)
    i2 = jnp.full_like(gbest, -1)
    v2 = jnp.full_like(v1, -jnp.inf)
    for i in range(n_per):
        better = (i1 != i) & (sv[i] > v2)
        i2 = jnp.where(better, i, i2)
        v2 = jnp.where(better, sv[i], v2)
    lo = jnp.minimum(i1, i2)
    hi = jnp.maximum(i1, i2)
    pair = jnp.where(lo == 0, hi - 1, jnp.where(lo == 1, hi + 1, PAIRS_PER_GROUP - 1))
    return gbest * PAIRS_PER_GROUP + pair


def _post_mix_kernel(x_ref, o_ref, w_ref, g_ref, b_ref, rwh_ref, rwl_ref, rb_ref, y_ref, cls_ref):
    z = DEEPNORM_ALPHA * x_ref[...] + _dot(o_ref[...], w_ref[...])
    y = _layer_norm(z, g_ref[...], b_ref[...])
    y_ref[...] = y
    yh, yl = _split_bf16(y)
    logits = _dot(yh, rwh_ref[...]) + _dot(yl, rwh_ref[...]) + _dot(yh, rwl_ref[...])
    cls_ref[...] = _route_class(jnp.transpose(logits)[0:N_EXPERTS, :], rb_ref[...])


def _post_mix(x, o, w_out_bf16, ln_g, ln_b, rw_hi, rw_lo, rb_col, tm):
    t, d = o.shape
    row = pl.BlockSpec((tm, d), lambda i: (i, 0))
    vec = pl.BlockSpec((1, d), lambda i: (0, 0))
    return pl.pallas_call(
        _post_mix_kernel,
        grid=(t // tm,),
        in_specs=[row, row,
                  pl.BlockSpec((d, d), lambda i: (0, 0)),
                  vec, vec,
                  pl.BlockSpec((d, LANES), lambda i: (0, 0)),
                  pl.BlockSpec((d, LANES), lambda i: (0, 0)),
                  pl.BlockSpec((N_EXPERTS, 1), lambda i: (0, 0))],
        out_specs=[row, pl.BlockSpec((1, tm), lambda i: (0, i))],
        out_shape=[jax.ShapeDtypeStruct((t, d), F32), jax.ShapeDtypeStruct((1, t), I32)],
        compiler_params=_cparams("parallel"),
        name="post_mix",
    )(x, o, w_out_bf16, ln_g, ln_b, rw_hi, rw_lo, rb_col)


def _plan_kernel(cls_ref, pos_ref, cnt_ref, cnt_s, base_s, run_s, *, tile):
    phase, i = pl.program_id(0), pl.program_id(1)
    tb = cls_ref.shape[1]
    onehot = cls_ref[...] == lax.broadcasted_iota(I32, (CLASS_ROWS, tb), 0)
    ohf = jnp.where(onehot, 1.0, 0.0)
    block_cnt = jnp.sum(ohf, axis=1, keepdims=True)

    @pl.when((phase == 0) & (i == 0))
    def _():
        cnt_s[...] = jnp.zeros_like(cnt_s)

    @pl.when(phase == 0)
    def _():
        cnt_s[...] += block_cnt

    @pl.when((phase == 1) & (i == 0))
    def _():
        cnt = cnt_s[...]
        ntile = jnp.floor((cnt + (tile - 1)) * (1.0 / tile))
        nt_hi = jnp.floor(ntile * (1.0 / 256.0))
        nt_lo = ntile - 256.0 * nt_hi
        r = lax.broadcasted_iota(I32, (CLASS_ROWS, CLASS_ROWS), 0)
        c = lax.broadcasted_iota(I32, (CLASS_ROWS, CLASS_ROWS), 1)
        below = jnp.where(c < r, 1.0, 0.0).astype(BF16)
        wide = (CLASS_ROWS, LANES)
        excl = (256.0 * _dot(below, jnp.broadcast_to(nt_hi, wide).astype(BF16))
                + _dot(below, jnp.broadcast_to(nt_lo, wide).astype(BF16)))
        base_s[...] = excl[:, 0:1] * tile
        run_s[...] = jnp.zeros_like(run_s)
        cnt_ref[...] = cnt

    @pl.when(phase == 1)
    def _():
        r = lax.broadcasted_iota(I32, (tb, tb), 0)
        c = lax.broadcasted_iota(I32, (tb, tb), 1)
        upto = jnp.where(r <= c, 1.0, 0.0).astype(BF16)
        within = _dot(ohf.astype(BF16), upto)
        rank = within + (run_s[...] + base_s[...] - 1.0)
        pos_ref[...] = jnp.sum(jnp.where(onehot, rank, 0.0), axis=0, keepdims=True).astype(I32)
        run_s[...] += block_cnt


def _plan(cls, tb, tile):
    t = cls.shape[1]
    return pl.pallas_call(
        functools.partial(_plan_kernel, tile=tile),
        grid=(2, t // tb),
        in_specs=[pl.BlockSpec((1, tb), lambda p, i: (0, i))],
        out_specs=[pl.BlockSpec((1, tb), lambda p, i: (0, i * p)),
                   pl.BlockSpec((CLASS_ROWS, 1), lambda p, i: (0, 0))],
        out_shape=[jax.ShapeDtypeStruct((1, t), I32), jax.ShapeDtypeStruct((CLASS_ROWS, 1), F32)],
        scratch_shapes=[pltpu.VMEM((CLASS_ROWS, 1), F32)] * 3,
        compiler_params=_cparams("arbitrary", "arbitrary"),
        name="moe_plan",
    )(cls)


def _invert_kernel(pos_ref, src_ref):
    def clear(i, carry):
        src_ref[i] = 0
        return carry

    def place(t, carry):
        src_ref[pos_ref[t]] = t
        return carry

    lax.fori_loop(0, src_ref.shape[0], clear, 0, unroll=8)
    lax.fori_loop(0, pos_ref.shape[0], place, 0, unroll=8)


def _invert(pos, n_sorted):
    return pl.pallas_call(
        _invert_kernel,
        grid_spec=pltpu.PrefetchScalarGridSpec(
            num_scalar_prefetch=1, grid=(1,), in_specs=[],
            out_specs=pl.BlockSpec(memory_space=pltpu.SMEM)),
        out_shape=jax.ShapeDtypeStruct((n_sorted,), I32),
        compiler_params=_cparams("arbitrary"),
        name="moe_invert",
    )(pos)


def _start_row_gather(hbm, idx_ref, base, buf, slot, sem, n):
    for j in range(n):
        pltpu.make_async_copy(hbm.at[pl.ds(idx_ref[base + j], 1)],
                              buf.at[slot, pl.ds(j, 1)], sem).start()


def _wait_row_gather(hbm, buf, slot, sem, n):
    for j in range(n):
        pltpu.make_async_copy(hbm.at[pl.ds(0, 1)], buf.at[slot, pl.ds(j, 1)], sem).wait()


def _experts_kernel(ea_ref, eb_ref, nvalid_ref, src_ref, x_hbm, rwa_ref, rwb_ref,
                    wga_ref, wua_ref, wda_ref, wgb_ref, wub_ref, wdb_ref, ys_ref, xbuf, sems,
                    *, tile):
    i = pl.program_id(0)
    nvalid = nvalid_ref[0]
    slot = i % 2

    @pl.when(i == 0)
    def _():
        _start_row_gather(x_hbm, src_ref, 0, xbuf, 0, sems.at[0], tile)

    @pl.when(i < nvalid)
    def _():
        _wait_row_gather(x_hbm, xbuf, slot, sems.at[slot], tile)
        nxt = jnp.minimum(i + 1, nvalid - 1)
        _start_row_gather(x_hbm, src_ref, nxt * tile, xbuf, 1 - slot, sems.at[1 - slot], tile)
        xf = xbuf[slot]
        x = xf.astype(BF16)
        dlogit = jnp.sum(xf * (rwa_ref[0] - rwb_ref[0]), axis=-1, keepdims=True)
        g_lo = 1.0 / (1.0 + jnp.exp(-dlogit))
        g_hi = 1.0 / (1.0 + jnp.exp(dlogit))
        ha = jax.nn.silu(_dot(x, wga_ref[0])) * _dot(x, wua_ref[0]) * g_lo
        hb = jax.nn.silu(_dot(x, wgb_ref[0])) * _dot(x, wub_ref[0]) * g_hi
        ys_ref[...] = _dot(ha.astype(BF16), wda_ref[0]) + _dot(hb.astype(BF16), wdb_ref[0])

    @pl.when(i == nvalid - 1)
    def _():
        _wait_row_gather(x_hbm, xbuf, 1 - slot, sems.at[1 - slot], tile)

    @pl.when(i >= nvalid)
    def _():
        ys_ref[...] = jnp.zeros_like(ys_ref)


def _experts(ea, eb, nvalid, src, x, rw_rows, w_gate, w_up, w_down, tile):
    n_sorted = src.shape[0]
    d, f = D_MODEL, D_EXPERT
    rw_a = pl.BlockSpec((1, 1, d), lambda i, ea, eb, nv, src: (ea[i], 0, 0))
    rw_b = pl.BlockSpec((1, 1, d), lambda i, ea, eb, nv, src: (eb[i], 0, 0))
    up_a = pl.BlockSpec((1, d, f), lambda i, ea, eb, nv, src: (ea[i], 0, 0))
    up_b = pl.BlockSpec((1, d, f), lambda i, ea, eb, nv, src: (eb[i], 0, 0))
    dn_a = pl.BlockSpec((1, f, d), lambda i, ea, eb, nv, src: (ea[i], 0, 0))
    dn_b = pl.BlockSpec((1, f, d), lambda i, ea, eb, nv, src: (eb[i], 0, 0))
    return pl.pallas_call(
        functools.partial(_experts_kernel, tile=tile),
        grid_spec=pltpu.PrefetchScalarGridSpec(
            num_scalar_prefetch=4, grid=(n_sorted // tile,),
            in_specs=[pl.BlockSpec(memory_space=pl.ANY), rw_a, rw_b,
                      up_a, up_a, dn_a, up_b, up_b, dn_b],
            out_specs=pl.BlockSpec((tile, d), lambda i, ea, eb, nv, src: (i, 0)),
            scratch_shapes=[pltpu.VMEM((2, tile, d), F32), pltpu.SemaphoreType.DMA((2,))]),
        out_shape=jax.ShapeDtypeStruct((n_sorted, d), F32),
        compiler_params=_cparams("arbitrary"),
        name="moe_experts",
    )(ea, eb, nvalid, src, x, rw_rows, rw_rows, w_gate, w_up, w_down, w_gate, w_up, w_down)


def _ln2_kernel(pos_ref, x_ref, ys_hbm, g_ref, b_ref, *rest, tile, with_qkv):
    if with_qkv:
        w_ref, o_ref, q_ref, k_ref, v_ref, ybuf, sems = rest
    else:
        o_ref, ybuf, sems = rest
    i = pl.program_id(0)
    n = pl.num_programs(0)
    slot = i % 2

    @pl.when(i == 0)
    def _():
        _start_row_gather(ys_hbm, pos_ref, 0, ybuf, 0, sems.at[0], tile)

    _wait_row_gather(ys_hbm, ybuf, slot, sems.at[slot], tile)
    nxt = jnp.minimum(i + 1, n - 1)
    _start_row_gather(ys_hbm, pos_ref, nxt * tile, ybuf, 1 - slot, sems.at[1 - slot], tile)
    y = _layer_norm(DEEPNORM_ALPHA * x_ref[...] + ybuf[slot], g_ref[...], b_ref[...])
    o_ref[...] = y
    if with_qkv:
        d = D_MODEL
        yb = y.astype(BF16)
        q_ref[...] = (_dot(yb, w_ref[:, 0:d]) * (DIFF_DH ** -0.5)).astype(BF16)
        k_ref[...] = _dot(yb, w_ref[:, d:2 * d]).astype(BF16)
        v_ref[...] = _dot(yb, w_ref[:, 2 * d:3 * d]).astype(BF16)

    @pl.when(i == n - 1)
    def _():
        _wait_row_gather(ys_hbm, ybuf, 1 - slot, sems.at[1 - slot], tile)


def _ln2(pos, x1, ys, ln_g, ln_b, w_qkv_bf16, tile):
    t, d = x1.shape
    with_qkv = w_qkv_bf16 is not None
    row = pl.BlockSpec((tile, d), lambda i, pos: (i, 0))
    vec = pl.BlockSpec((1, d), lambda i, pos: (0, 0))
    in_specs = [row, pl.BlockSpec(memory_space=pl.ANY), vec, vec]
    out_specs = [row]
    out_shape = [jax.ShapeDtypeStruct((t, d), F32)]
    args = [pos, x1, ys, ln_g, ln_b]
    if with_qkv:
        in_specs.append(pl.BlockSpec((d, 3 * d), lambda i, pos: (0, 0)))
        out_specs += [row, row, row]
        out_shape += [jax.ShapeDtypeStruct((t, d), BF16)] * 3
        args.append(w_qkv_bf16)
    return pl.pallas_call(
        functools.partial(_ln2_kernel, tile=tile, with_qkv=with_qkv),
        grid_spec=pltpu.PrefetchScalarGridSpec(
            num_scalar_prefetch=1, grid=(t // tile,),
            in_specs=in_specs, out_specs=out_specs,
            scratch_shapes=[pltpu.VMEM((2, tile, d), F32), pltpu.SemaphoreType.DMA((2,))]),
        out_shape=out_shape,
        compiler_params=_cparams("arbitrary"),
        name="ln2_qkv" if with_qkv else "ln2",
    )(*args)


def _moe_layer(x1, cls, rw_rows, w_gate, w_up, w_down, ln_g, ln_b, w_qkv_bf16):
    t = x1.shape[0]
    tile = min(MOE_TILE, t)
    n_tiles = t // tile + N_CLASSES
    pos, cnt = _plan(cls, min(512, t), tile)
    pos = pos.reshape(t)
    cnt = cnt[:N_CLASSES, 0].astype(I32)
    ntile = (cnt + tile - 1) // tile
    tile_end = jnp.cumsum(ntile)
    nvalid = tile_end[-1]
    tid = jnp.minimum(jnp.arange(n_tiles, dtype=I32), nvalid - 1)
    tcls = jnp.sum((tid[:, None] >= tile_end[None, :]).astype(I32), axis=1)
    grp, pair = tcls // PAIRS_PER_GROUP, tcls % PAIRS_PER_GROUP
    ea = grp * EXPERTS_PER_GROUP + jnp.array([0, 0, 0, 1, 1, 2], I32)[pair]
    eb = grp * EXPERTS_PER_GROUP + jnp.array([1, 2, 3, 2, 3, 3], I32)[pair]
    src = _invert(pos, n_tiles * tile)
    ys = _experts(ea, eb, nvalid.reshape(1), src, x1, rw_rows, w_gate, w_up, w_down, tile)
    return _ln2(pos, x1, ys, ln_g, ln_b, w_qkv_bf16, min(LN2_TILE, t))


def _diff_attn_kernel(q_ref, k_ref, v_ref, lam_ref, sg_ref, o_ref, m_s, acc_s, *, tq, lambda_init):
    qi = pl.program_id(2)
    dh, dv = DIFF_DH, DIFF_DV
    heads = ATTN_HEADS_PER_STEP
    rows = 2 * tq
    lane = lax.broadcasted_iota(I32, (tq, dv), 1)
    qs = []
    for h in range(heads):
        q = q_ref[:, h * dv:(h + 1) * dv]
        zero = jnp.zeros_like(q)
        qs.append(jnp.concatenate([jnp.where(lane < dh, q, zero), jnp.where(lane >= dh, q, zero)],
                                  axis=0))
    m_s[...] = jnp.full_like(m_s, NEG_BIG)
    acc_s[...] = jnp.zeros_like(acc_s)
    ones = jnp.ones((1, dv), BF16)

    def block(r0, width, masked):
        ncol = width // LANES
        for h in range(heads):
            s = _dot_nt(qs[h], k_ref[pl.ds(r0, width), h * dv:(h + 1) * dv])
            if masked:
                row = lax.broadcasted_iota(I32, (tq, width), 0)
                col = lax.broadcasted_iota(I32, (tq, width), 1)
                keep = jnp.concatenate([col <= row, col <= row], axis=0)
                s = jnp.where(keep, s, NEG_BIG)
            cols = [s[:, c * LANES:(c + 1) * LANES] for c in range(ncol)]
            part = cols[0]
            for c in range(1, ncol):
                part = jnp.maximum(part, cols[c])
            m_old = m_s[h]
            m_new = jnp.maximum(m_old, jnp.broadcast_to(jnp.max(part, axis=-1, keepdims=True),
                                                        (rows, LANES)))
            p = jnp.concatenate([jnp.exp(cb - m_new).astype(BF16) for cb in cols], axis=1)
            v = v_ref[pl.ds(r0, width), h * dv:(h + 1) * dv]
            v_ext = jnp.concatenate([v, jnp.broadcast_to(ones, (width, dv))], axis=1)
            scale = jnp.exp(m_old - m_new)
            acc_s[h] = acc_s[h] * jnp.concatenate([scale, scale], axis=1) + _dot(p, v_ext)
            m_s[h] = m_new

    wide = 2 * tq

    def body(j, carry):
        block(pl.multiple_of(j * wide, wide), wide, False)
        return carry

    lax.fori_loop(0, qi // 2, body, 0)

    @pl.when(qi % 2 == 1)
    def _():
        block(pl.multiple_of((qi - 1) * tq, tq), tq, False)

    block(pl.multiple_of(qi * tq, tq), tq, True)

    lam = lam_ref[...]
    lam_val = (jnp.exp(jnp.sum(lam[0:1] * lam[1:2], axis=1, keepdims=True))
               - jnp.exp(jnp.sum(lam[2:3] * lam[3:4], axis=1, keepdims=True)) + lambda_init)
    for h in range(heads):
        acc = acc_s[h]
        o12 = acc[:, 0:dv] / acc[:, dv:2 * dv]
        o = o12[0:tq] - lam_val * o12[tq:rows]
        o = o * lax.rsqrt(jnp.mean(o * o, axis=-1, keepdims=True) + RMS_EPS) * sg_ref[...]
        o_ref[:, h * dv:(h + 1) * dv] = (o * (1.0 - lambda_init)).astype(BF16)


def _diff_attn(q, k, v, lam, subln_g, batch, seq, tq, lambda_init):
    t, d = q.shape
    nq = seq // tq
    heads = ATTN_HEADS_PER_STEP
    w = heads * DIFF_DV
    return pl.pallas_call(
        functools.partial(_diff_attn_kernel, tq=tq, lambda_init=lambda_init),
        grid=(batch, DIFF_HEADS // heads, nq),
        in_specs=[pl.BlockSpec((tq, w), lambda b, h, i: (b * nq + i, h)),
                  pl.BlockSpec((seq, w), lambda b, h, i: (b, h)),
                  pl.BlockSpec((seq, w), lambda b, h, i: (b, h)),
                  pl.BlockSpec(lam.shape, lambda b, h, i: (0, 0)),
                  pl.BlockSpec((1, DIFF_DV), lambda b, h, i: (0, 0))],
        out_specs=pl.BlockSpec((tq, w), lambda b, h, i: (b * nq + i, h)),
        out_shape=jax.ShapeDtypeStruct((t, d), BF16),
        scratch_shapes=[pltpu.VMEM((heads, 2 * tq, LANES), F32),
                        pltpu.VMEM((heads, 2 * tq, 2 * DIFF_DV), F32)],
        compiler_params=_cparams("parallel", "parallel", "arbitrary"),
        name="diff_attn",
    )(q, k, v, lam, subln_g)


def kernel(x, a_w_in, a_lb, a_norm_g, a_w_out, kv_w, b_w_q, b_lam, b_subln_g, b_w_out,
           ln1_g, ln1_b, ln2_g, ln2_b, router_w, router_b, moe_w_gate, moe_w_up, moe_w_down):
    batch, seq, d = x.shape
    t = batch * seq
    tm = min(512, t)
    xt = x.reshape(t, d)

    rw = jnp.pad(router_w, ((0, 0), (0, LANES - N_EXPERTS)))
    rw_hi = rw.astype(BF16)
    rw_lo = (rw - rw_hi.astype(F32)).astype(BF16)
    rb_col = router_b.reshape(N_EXPERTS, 1)
    rw_rows = router_w.T.reshape(N_EXPERTS, 1, d)
    w_qkv = jnp.concatenate([b_w_q[0], kv_w], axis=1).astype(BF16)

    qkv = None
    for layer in range(DEPTH):
        if layer < N_A_LAYERS:
            q, k, v, g, lf = _hgrn_in(xt, a_w_in[layer].astype(BF16), a_lb, layer, tm)
            o = _hgrn_scan(q, k, v, lf, g, a_norm_g[layer], batch, seq, min(512, seq))
            w_out = a_w_out[layer]
        else:
            j = layer - N_A_LAYERS
            lambda_init = 0.8 - 0.6 * math.exp(-0.3 * layer)
            o = _diff_attn(*qkv, b_lam[j], b_subln_g[j].reshape(1, DIFF_DV), batch, seq,
                           min(ATTN_TQ, seq), lambda_init)
            w_out = b_w_out[j]
        x1, cls = _post_mix(xt, o, w_out.astype(BF16), ln1_g[layer].reshape(1, d),
                            ln1_b[layer].reshape(1, d), rw_hi, rw_lo, rb_col, tm)
        outs = _moe_layer(x1, cls, rw_rows, moe_w_gate[layer].astype(BF16),
                          moe_w_up[layer].astype(BF16), moe_w_down[layer].astype(BF16),
                          ln2_g[layer].reshape(1, d), ln2_b[layer].reshape(1, d),
                          w_qkv if layer + 1 == N_A_LAYERS else None)
        xt, qkv = outs[0], outs[1:]
    return xt.reshape(batch, seq, d)
```

```python
import functools
import math

import jax
import jax.numpy as jnp
from jax import lax
from jax.experimental import pallas as pl
from jax.experimental.pallas import tpu as pltpu

F32 = jnp.float32
BF16 = jnp.bfloat16
I32 = jnp.int32

D_MODEL = 1024
DEPTH = 2
N_A_LAYERS = DEPTH // 2
HGRN_HEADS = 8
HGRN_DH = D_MODEL // HGRN_HEADS
HGRN_CHUNK = 64
DIFF_HEADS = 8
DIFF_DH = D_MODEL // (2 * DIFF_HEADS)
DIFF_DV = 2 * DIFF_DH
N_EXPERTS = 16
N_GROUPS = 4
EXPERTS_PER_GROUP = N_EXPERTS // N_GROUPS
PAIRS_PER_GROUP = 6
N_CLASSES = N_GROUPS * PAIRS_PER_GROUP
CLASS_ROWS = 32
D_EXPERT = 512
DEEPNORM_ALPHA = (2.0 * DEPTH) ** 0.25
LN_EPS = 1e-5
RMS_EPS = 1e-6

LANES = 128
MOE_TILE = 256
LN2_TILE = 256
ATTN_TQ = 256
ATTN_HEADS_PER_STEP = 4
NEG_BIG = -1e30
VMEM_LIMIT = 56 * 1024 * 1024


def _cparams(*sem):
    return pltpu.CompilerParams(dimension_semantics=sem, vmem_limit_bytes=VMEM_LIMIT)


def _dot(a, b):
    return jnp.dot(a, b, preferred_element_type=F32)


def _dot_nt(a, b):
    return lax.dot_general(a, b, (((1,), (1,)), ((), ())), preferred_element_type=F32)


def _dot_tn(a, b):
    return lax.dot_general(a, b, (((0,), (0,)), ((), ())), preferred_element_type=F32)


def _split_bf16(x):
    hi = x.astype(BF16)
    lo = (x - hi.astype(F32)).astype(BF16)
    return hi, lo


def _layer_norm(z, g, b):
    mu = jnp.mean(z, axis=-1, keepdims=True)
    zc = z - mu
    var = jnp.mean(zc * zc, axis=-1, keepdims=True)
    return zc * lax.rsqrt(var + LN_EPS) * g + b


def _hgrn_in_kernel(x_ref, w_ref, lb_ref, q_ref, k_ref, v_ref, g_ref, lf_ref, *, layer):
    d = D_MODEL
    x = x_ref[...].astype(BF16)
    a = lb_ref[...]
    e = jnp.exp(a - jnp.max(a, axis=0, keepdims=True))
    sm = e / jnp.sum(e, axis=0, keepdims=True)
    lb = jnp.sum(sm[0:layer + 1], axis=0, keepdims=True)
    q_ref[...] = _dot(x, w_ref[:, 0:d]).astype(BF16)
    f = lb + (1.0 - lb) * jax.nn.sigmoid(_dot(x, w_ref[:, d:2 * d]))
    lf_ref[...] = jnp.log(f)
    k_ref[...] = (1.0 - f).astype(BF16)
    v_ref[...] = _dot(x, w_ref[:, 2 * d:3 * d]).astype(BF16)
    g_ref[...] = jax.nn.sigmoid(_dot(x, w_ref[:, 3 * d:4 * d])).astype(BF16)


def _hgrn_in(x, w_in_bf16, a_lb, layer, tm):
    t, d = x.shape
    row = pl.BlockSpec((tm, d), lambda i: (i, 0))
    return pl.pallas_call(
        functools.partial(_hgrn_in_kernel, layer=layer),
        grid=(t // tm,),
        in_specs=[row,
                  pl.BlockSpec((d, 4 * d), lambda i: (0, 0)),
                  pl.BlockSpec(a_lb.shape, lambda i: (0, 0))],
        out_specs=[row, row, row, row, row],
        out_shape=[jax.ShapeDtypeStruct((t, d), BF16)] * 4 + [jax.ShapeDtypeStruct((t, d), F32)],
        compiler_params=_cparams("parallel"),
        name="hgrn_in",
    )(x, w_in_bf16, a_lb)


def _hgrn_scan_kernel(q_ref, k_ref, v_ref, lf_ref, g_ref, ng_ref, o_ref, state_ref, *, n_chunks):
    c_len, dh = HGRN_CHUNK, HGRN_DH

    @pl.when(pl.program_id(1) == 0)
    def _():
        state_ref[...] = jnp.zeros_like(state_ref)

    row = lax.broadcasted_iota(I32, (c_len, c_len), 0)
    col = lax.broadcasted_iota(I32, (c_len, c_len), 1)
    causal = col <= row
    tri = jnp.where(causal, 1.0, 0.0).astype(BF16)

    heads = range(HGRN_HEADS)
    cols = [slice(h * dh, (h + 1) * dh) for h in heads]

    def chunk(c, carry):
        r0 = pl.multiple_of(c * c_len, c_len)
        rows = pl.ds(r0, c_len)
        lf_hi, lf_lo = _split_bf16(lf_ref[rows, :])
        b_all = _dot(tri, lf_hi) + _dot(tri, lf_lo)
        b = [b_all[:, cs] for cs in cols]
        q = [q_ref[rows, cs].astype(F32) for cs in cols]
        k = [k_ref[rows, cs].astype(F32) for cs in cols]
        v = [v_ref[rows, cs] for cs in cols]
        b_mid = [x[c_len // 2 - 1:c_len // 2, :] for x in b]
        b_last = [x[c_len - 1:c_len, :] for x in b]
        e_up = [jnp.exp(b[h] - b_mid[h]) for h in heads]
        e_dn = [jnp.exp(b_mid[h] - b[h]) for h in heads]
        qs = [q[h] * e_up[h] for h in heads]
        ks = [k[h] * e_dn[h] for h in heads]
        s = [jnp.where(causal, _dot_nt(qs[h].astype(BF16), ks[h].astype(BF16)), 0.0).astype(BF16)
             for h in heads]
        st = [state_ref[h] for h in heads]
        qd = [(qs[h] * jnp.exp(b_mid[h])).astype(BF16) for h in heads]
        o = [_dot(s[h], v[h]) + _dot_nt(qd[h], st[h].astype(BF16)) for h in heads]
        kd = [(ks[h] * jnp.exp(b_last[h] - b_mid[h])).astype(BF16) for h in heads]
        for h in heads:
            state_ref[h] = jnp.exp(b_last[h]) * st[h] + _dot_tn(v[h], kd[h])
        for h in heads:
            on = o[h] * lax.rsqrt(jnp.mean(o[h] * o[h], axis=-1, keepdims=True) + RMS_EPS)
            o_ref[rows, cols[h]] = (on * ng_ref[h:h + 1, :]
                                    * g_ref[rows, cols[h]].astype(F32)).astype(BF16)
        return carry

    lax.fori_loop(0, n_chunks, chunk, 0)


def _hgrn_scan(q, k, v, lf, g, norm_g, batch, seq, rows_per_step):
    t, d = q.shape
    steps = seq // rows_per_step
    blk = pl.BlockSpec((rows_per_step, d), lambda b, s: (b * steps + s, 0))
    return pl.pallas_call(
        functools.partial(_hgrn_scan_kernel, n_chunks=rows_per_step // HGRN_CHUNK),
        grid=(batch, steps),
        in_specs=[blk, blk, blk, blk, blk, pl.BlockSpec(norm_g.shape, lambda b, s: (0, 0))],
        out_specs=blk,
        out_shape=jax.ShapeDtypeStruct((t, d), BF16),
        scratch_shapes=[pltpu.VMEM((HGRN_HEADS, HGRN_DH, HGRN_DH), F32)],
        compiler_params=_cparams("parallel", "arbitrary"),
        name="hgrn_scan",
    )(q, k, v, lf, g, norm_g)


def _route_class(logits_t, rb):
    e = jnp.exp(logits_t - jnp.max(logits_t, axis=0, keepdims=True))
    sel = e / jnp.sum(e, axis=0, keepdims=True) + rb
    selr = [sel[i:i + 1, :] for i in range(N_EXPERTS)]
    n_per = EXPERTS_PER_GROUP

    def top2_sum(vals):
        best = None
        for i in range(len(vals)):
            for j in range(i + 1, len(vals)):
                pair = vals[i] + vals[j]
                best = pair if best is None else jnp.maximum(best, pair)
        return best

    gscore = [top2_sum(selr[g * n_per:(g + 1) * n_per]) for g in range(N_GROUPS)]
    gbest = jnp.zeros_like(gscore[0], dtype=I32)
    gval = gscore[0]
    for g in range(1, N_GROUPS):
        better = gscore[g] > gval
        gbest = jnp.where(better, g, gbest)
        gval = jnp.where(better, gscore[g], gval)

    def in_best_group(i):
        out = selr[(N_GROUPS - 1) * n_per + i]
        for g in range(N_GROUPS - 2, -1, -1):
            out = jnp.where(gbest == g, selr[g * n_per + i], out)
        return out

    sv = [in_best_group(i) for i in range(n_per)]
    i1 = jnp.zeros_like(gbest)
    v1 = sv[0]
    for i in range(1, n_per):
        better = sv[i] > v1
        i1 = jnp.where(better, i, i1)
        v1 = jnp.where(better, sv[i], v1)
    i2 = jnp.full_like(gbest, -1)
    v2 = jnp.full_like(v1, -jnp.inf)
    for i in range(n_per):
        better = (i1 != i) & (sv[i] > v2)
        i2 = jnp.where(better, i, i2)
        v2 = jnp.where(better, sv[i], v2)
    lo = jnp.minimum(i1, i2)
    hi = jnp.maximum(i1, i2)
    pair = jnp.where(lo == 0, hi - 1, jnp.where(lo == 1, hi + 1, PAIRS_PER_GROUP - 1))
    return gbest * PAIRS_PER_GROUP + pair


def _post_mix_kernel(x_ref, o_ref, w_ref, g_ref, b_ref, rw_ref, rb_ref, y_ref, yt_ref, cls_ref,
                     *, sub):
    nchunk = D_MODEL // LANES
    for r0 in range(0, x_ref.shape[0], sub):
        rows = slice(r0, r0 + sub)
        z = DEEPNORM_ALPHA * x_ref[rows, :] + _dot(o_ref[rows, :], w_ref[...])
        y = _layer_norm(z, g_ref[...], b_ref[...])
        y_ref[rows, :] = y
        for c in range(nchunk):
            yt_ref[pl.ds(r0 * nchunk + c, sub, stride=nchunk), :] = y[:, c * LANES:(c + 1) * LANES]
        yh, yl = _split_bf16(y)
        both = _dot(yh, rw_ref[...])
        logits = both[:, 0:LANES] + both[:, LANES:2 * LANES] + _dot(yl, rw_ref[:, 0:LANES])
        cls_ref[:, rows] = _route_class(jnp.transpose(logits)[0:N_EXPERTS, :], rb_ref[...])


def _post_mix(x, o, w_out_bf16, ln_g, ln_b, rw_cat, rb_col, tm):
    t, d = o.shape
    row = pl.BlockSpec((tm, d), lambda i: (i, 0))
    vec = pl.BlockSpec((1, d), lambda i: (0, 0))
    return pl.pallas_call(
        functools.partial(_post_mix_kernel, sub=min(256, tm)),
        grid=(t // tm,),
        in_specs=[row, row,
                  pl.BlockSpec((d, d), lambda i: (0, 0)),
                  vec, vec,
                  pl.BlockSpec((d, 2 * LANES), lambda i: (0, 0)),
                  pl.BlockSpec((N_EXPERTS, 1), lambda i: (0, 0))],
        out_specs=[row, pl.BlockSpec((tm * (d // LANES), LANES), lambda i: (i, 0)),
                   pl.BlockSpec((1, tm), lambda i: (0, i))],
        out_shape=[jax.ShapeDtypeStruct((t, d), F32),
                   jax.ShapeDtypeStruct((t * (d // LANES), LANES), F32),
                   jax.ShapeDtypeStruct((1, t), I32)],
        compiler_params=_cparams("parallel"),
        name="post_mix",
    )(x, o, w_out_bf16, ln_g, ln_b, rw_cat, rb_col)


def _plan_kernel(cls_ref, pos_ref, cnt_ref, cnt_s, base_s, run_s, *, tile):
    phase, i = pl.program_id(0), pl.program_id(1)
    tb = cls_ref.shape[1]
    onehot = cls_ref[...] == lax.broadcasted_iota(I32, (CLASS_ROWS, tb), 0)
    ohf = jnp.where(onehot, 1.0, 0.0)
    block_cnt = jnp.sum(ohf, axis=1, keepdims=True)

    @pl.when((phase == 0) & (i == 0))
    def _():
        cnt_s[...] = jnp.zeros_like(cnt_s)

    @pl.when(phase == 0)
    def _():
        cnt_s[...] += block_cnt

    @pl.when((phase == 1) & (i == 0))
    def _():
        cnt = cnt_s[...]
        ntile = jnp.floor((cnt + (tile - 1)) * (1.0 / tile))
        nt_hi = jnp.floor(ntile * (1.0 / 256.0))
        nt_lo = ntile - 256.0 * nt_hi
        r = lax.broadcasted_iota(I32, (CLASS_ROWS, CLASS_ROWS), 0)
        c = lax.broadcasted_iota(I32, (CLASS_ROWS, CLASS_ROWS), 1)
        below = jnp.where(c < r, 1.0, 0.0).astype(BF16)
        wide = (CLASS_ROWS, LANES)
        excl = (256.0 * _dot(below, jnp.broadcast_to(nt_hi, wide).astype(BF16))
                + _dot(below, jnp.broadcast_to(nt_lo, wide).astype(BF16)))
        base_s[...] = excl[:, 0:1] * tile
        run_s[...] = jnp.zeros_like(run_s)
        cnt_ref[...] = cnt

    @pl.when(phase == 1)
    def _():
        r = lax.broadcasted_iota(I32, (tb, tb), 0)
        c = lax.broadcasted_iota(I32, (tb, tb), 1)
        upto = jnp.where(r <= c, 1.0, 0.0).astype(BF16)
        within = _dot(ohf.astype(BF16), upto)
        rank = within + (run_s[...] + base_s[...] - 1.0)
        pos_ref[...] = jnp.sum(jnp.where(onehot, rank, 0.0), axis=0, keepdims=True).astype(I32)
        run_s[...] += block_cnt


def _plan(cls, tb, tile):
    t = cls.shape[1]
    return pl.pallas_call(
        functools.partial(_plan_kernel, tile=tile),
        grid=(2, t // tb),
        in_specs=[pl.BlockSpec((1, tb), lambda p, i: (0, i))],
        out_specs=[pl.BlockSpec((1, tb), lambda p, i: (0, i * p)),
                   pl.BlockSpec((CLASS_ROWS, 1), lambda p, i: (0, 0))],
        out_shape=[jax.ShapeDtypeStruct((1, t), I32), jax.ShapeDtypeStruct((CLASS_ROWS, 1), F32)],
        scratch_shapes=[pltpu.VMEM((CLASS_ROWS, 1), F32)] * 3,
        compiler_params=_cparams("arbitrary", "arbitrary"),
        name="moe_plan",
    )(cls)


def _invert_kernel(pos_ref, src_ref):
    def clear(i, carry):
        src_ref[i] = 0
        return carry

    def place(t, carry):
        src_ref[pos_ref[t]] = t
        return carry

    lax.fori_loop(0, src_ref.shape[0], clear, 0, unroll=8)
    lax.fori_loop(0, pos_ref.shape[0], place, 0, unroll=8)


def _invert(pos, n_sorted):
    return pl.pallas_call(
        _invert_kernel,
        grid_spec=pltpu.PrefetchScalarGridSpec(
            num_scalar_prefetch=1, grid=(1,), in_specs=[],
            out_specs=pl.BlockSpec(memory_space=pltpu.SMEM)),
        out_shape=jax.ShapeDtypeStruct((n_sorted,), I32),
        compiler_params=_cparams("arbitrary"),
        name="moe_invert",
    )(pos)


def _start_row_gather(hbm, idx_ref, base, buf, slot, sem, lo, hi, rows=1):
    for j in range(lo, hi):
        src0 = idx_ref[base + j] * rows
        if rows > 1:
            src0 = pl.multiple_of(src0, rows)
        pltpu.make_async_copy(hbm.at[pl.ds(src0, rows)],
                              buf.at[slot, pl.ds(j * rows, rows)], sem).start()


def _row_groups(n, parts):
    edges = [n * p // parts for p in range(parts + 1)]
    return list(zip(edges[:-1], edges[1:]))


def _wait_row_gather(hbm, buf, slot, sem, n, rows=1):
    for j in range(n):
        pltpu.make_async_copy(hbm.at[pl.ds(0, rows)],
                              buf.at[slot, pl.ds(j * rows, rows)], sem).wait()


def _experts_kernel(ea_ref, eb_ref, nvalid_ref, src_ref, x_hbm, rwa_ref, rwb_ref,
                    wga_ref, wua_ref, wda_ref, wgb_ref, wub_ref, wdb_ref, ys_ref,
                    xbuf, xb_s, h_s, sems, *, tile):
    i = pl.program_id(0)
    nvalid = nvalid_ref[0]
    slot = i % 2

    nchunk = D_MODEL // LANES

    @pl.when(i == 0)
    def _():
        _start_row_gather(x_hbm, src_ref, 0, xbuf, 0, sems.at[0], 0, tile, nchunk)

    def run(slot):
        _wait_row_gather(x_hbm, xbuf, slot, sems.at[slot], tile, nchunk)
        base = jnp.minimum(i + 1, nvalid - 1) * tile
        groups = _row_groups(tile, 4)

        def prefetch(p):
            _start_row_gather(x_hbm, src_ref, base, xbuf, 1 - slot, sems.at[1 - slot], *groups[p],
                              nchunk)

        xcur = xbuf.at[slot]
        xf = jnp.concatenate([xcur[pl.ds(c, tile, stride=nchunk), :] for c in range(nchunk)], axis=1)
        dlogit = jnp.sum(xf * (rwa_ref[0] - rwb_ref[0]), axis=-1, keepdims=True)
        g_lo = 1.0 / (1.0 + jnp.exp(-dlogit))
        g_hi = 1.0 / (1.0 + jnp.exp(dlogit))
        xb_s[...] = xf.astype(BF16)
        prefetch(0)
        x = xb_s[...]
        h_s[:, 0:D_EXPERT] = (jax.nn.silu(_dot(x, wga_ref[0])) * _dot(x, wua_ref[0])
                              * g_lo).astype(BF16)
        prefetch(1)
        h_s[:, D_EXPERT:2 * D_EXPERT] = (jax.nn.silu(_dot(x, wgb_ref[0])) * _dot(x, wub_ref[0])
                                         * g_hi).astype(BF16)
        prefetch(2)
        ys_ref[...] = (_dot(h_s[:, 0:D_EXPERT], wda_ref[0])
                       + _dot(h_s[:, D_EXPERT:2 * D_EXPERT], wdb_ref[0]))
        prefetch(3)

    for s in (0, 1):
        pl.when((i < nvalid) & (slot == s))(functools.partial(run, s))

    @pl.when(i == nvalid - 1)
    def _():
        _wait_row_gather(x_hbm, xbuf, 1 - slot, sems.at[1 - slot], tile, nchunk)

    @pl.when(i >= nvalid)
    def _():
        ys_ref[...] = jnp.zeros_like(ys_ref)


def _experts(ea, eb, nvalid, src, x, rw_rows, w_gate, w_up, w_down, tile):
    n_sorted = src.shape[0]
    d, f = D_MODEL, D_EXPERT
    rw_a = pl.BlockSpec((1, 1, d), lambda i, ea, eb, nv, src: (ea[i], 0, 0))
    rw_b = pl.BlockSpec((1, 1, d), lambda i, ea, eb, nv, src: (eb[i], 0, 0))
    up_a = pl.BlockSpec((1, d, f), lambda i, ea, eb, nv, src: (ea[i], 0, 0))
    up_b = pl.BlockSpec((1, d, f), lambda i, ea, eb, nv, src: (eb[i], 0, 0))
    dn_a = pl.BlockSpec((1, f, d), lambda i, ea, eb, nv, src: (ea[i], 0, 0))
    dn_b = pl.BlockSpec((1, f, d), lambda i, ea, eb, nv, src: (eb[i], 0, 0))
    return pl.pallas_call(
        functools.partial(_experts_kernel, tile=tile),
        grid_spec=pltpu.PrefetchScalarGridSpec(
            num_scalar_prefetch=4, grid=(n_sorted // tile,),
            in_specs=[pl.BlockSpec(memory_space=pl.ANY), rw_a, rw_b,
                      up_a, up_a, dn_a, up_b, up_b, dn_b],
            out_specs=pl.BlockSpec((tile, d), lambda i, ea, eb, nv, src: (i, 0)),
            scratch_shapes=[pltpu.VMEM((2, tile * (d // LANES), LANES), F32),
                            pltpu.VMEM((tile, d), BF16),
                            pltpu.VMEM((tile, 2 * f), BF16),
                            pltpu.SemaphoreType.DMA((2,))]),
        out_shape=jax.ShapeDtypeStruct((n_sorted, d), F32),
        compiler_params=_cparams("arbitrary"),
        name="moe_experts",
    )(ea, eb, nvalid, src, x, rw_rows, rw_rows, w_gate, w_up, w_down, w_gate, w_up, w_down)


def _ln2_kernel(pos_ref, x_ref, ys_hbm, g_ref, b_ref, *rest, tile, with_qkv):
    if with_qkv:
        w_ref, o_ref, q_ref, k_ref, v_ref, ybuf, sems = rest
    else:
        o_ref, ybuf, sems = rest
    i = pl.program_id(0)
    n = pl.num_programs(0)
    slot = i % 2

    @pl.when(i == 0)
    def _():
        _start_row_gather(ys_hbm, pos_ref, 0, ybuf, 0, sems.at[0], 0, tile)

    def run(slot):
        _wait_row_gather(ys_hbm, ybuf, slot, sems.at[slot], tile)
        base = jnp.minimum(i + 1, n - 1) * tile
        groups = _row_groups(tile, 4 if with_qkv else 1)

        def prefetch(p):
            _start_row_gather(ys_hbm, pos_ref, base, ybuf, 1 - slot, sems.at[1 - slot], *groups[p])

        z = DEEPNORM_ALPHA * x_ref[...] + ybuf[slot]
        prefetch(0)
        y = _layer_norm(z, g_ref[...], b_ref[...])
        o_ref[...] = y
        if with_qkv:
            d = D_MODEL
            yb = y.astype(BF16)
            q_ref[...] = (_dot(yb, w_ref[:, 0:d]) * (DIFF_DH ** -0.5)).astype(BF16)
            prefetch(1)
            k_ref[...] = _dot(yb, w_ref[:, d:2 * d]).astype(BF16)
            prefetch(2)
            v_ref[...] = _dot(yb, w_ref[:, 2 * d:3 * d]).astype(BF16)
            prefetch(3)

    for s in (0, 1):
        pl.when(slot == s)(functools.partial(run, s))

    @pl.when(i == n - 1)
    def _():
        _wait_row_gather(ys_hbm, ybuf, 1 - slot, sems.at[1 - slot], tile)


def _ln2(pos, x1, ys, ln_g, ln_b, w_qkv_bf16, tile):
    t, d = x1.shape
    with_qkv = w_qkv_bf16 is not None
    row = pl.BlockSpec((tile, d), lambda i, pos: (i, 0))
    vec = pl.BlockSpec((1, d), lambda i, pos: (0, 0))
    in_specs = [row, pl.BlockSpec(memory_space=pl.ANY), vec, vec]
    out_specs = [row]
    out_shape = [jax.ShapeDtypeStruct((t, d), F32)]
    args = [pos, x1, ys, ln_g, ln_b]
    if with_qkv:
        in_specs.append(pl.BlockSpec((d, 3 * d), lambda i, pos: (0, 0)))
        out_specs += [row, row, row]
        out_shape += [jax.ShapeDtypeStruct((t, d), BF16)] * 3
        args.append(w_qkv_bf16)
    return pl.pallas_call(
        functools.partial(_ln2_kernel, tile=tile, with_qkv=with_qkv),
        grid_spec=pltpu.PrefetchScalarGridSpec(
            num_scalar_prefetch=1, grid=(t // tile,),
            in_specs=in_specs, out_specs=out_specs,
            scratch_shapes=[pltpu.VMEM((2, tile, d), F32), pltpu.SemaphoreType.DMA((2,))]),
        out_shape=out_shape,
        compiler_params=_cparams("arbitrary"),
        name="ln2_qkv" if with_qkv else "ln2",
    )(*args)


def _moe_layer(x1, x1t, cls, rw_rows, w_gate, w_up, w_down, ln_g, ln_b, w_qkv_bf16):
    t = x1.shape[0]
    tile = min(MOE_TILE, t)
    n_tiles = t // tile + N_CLASSES
    pos, cnt = _plan(cls, min(512, t), tile)
    pos = pos.reshape(t)
    cnt = cnt[:N_CLASSES, 0].astype(I32)
    ntile = (cnt + tile - 1) // tile
    tile_end = jnp.cumsum(ntile)
    nvalid = tile_end[-1]
    tid = jnp.minimum(jnp.arange(n_tiles, dtype=I32), nvalid - 1)
    tcls = jnp.sum((tid[:, None] >= tile_end[None, :]).astype(I32), axis=1)
    grp, pair = tcls // PAIRS_PER_GROUP, tcls % PAIRS_PER_GROUP
    ea = grp * EXPERTS_PER_GROUP + jnp.array([0, 0, 0, 1, 1, 2], I32)[pair]
    eb = grp * EXPERTS_PER_GROUP + jnp.array([1, 2, 3, 2, 3, 3], I32)[pair]
    src = _invert(pos, n_tiles * tile)
    ys = _experts(ea, eb, nvalid.reshape(1), src, x1t, rw_rows, w_gate, w_up, w_down, tile)
    return _ln2(pos, x1, ys, ln_g, ln_b, w_qkv_bf16, min(LN2_TILE, t))


def _diff_attn_kernel(q_ref, k_ref, v_ref, lam_ref, sg_ref, o_ref, m_s, acc_s, *, tq, lambda_init):
    qi = pl.program_id(2)
    dh, dv = DIFF_DH, DIFF_DV
    heads = ATTN_HEADS_PER_STEP
    rows = 2 * tq
    lane = lax.broadcasted_iota(I32, (tq, dv), 1)
    qs = []
    for h in range(heads):
        q = q_ref[:, h * dv:(h + 1) * dv]
        zero = jnp.zeros_like(q)
        qs.append(jnp.concatenate([jnp.where(lane < dh, q, zero), jnp.where(lane >= dh, q, zero)],
                                  axis=0))
    m_s[...] = jnp.full_like(m_s, NEG_BIG)
    acc_s[...] = jnp.zeros_like(acc_s)
    ones = jnp.ones((1, dv), BF16)

    def block(r0, width, masked):
        ncol = width // LANES
        for h in range(heads):
            s = _dot_nt(qs[h], k_ref[pl.ds(r0, width), h * dv:(h + 1) * dv])
            if masked:
                row = lax.broadcasted_iota(I32, (tq, width), 0)
                col = lax.broadcasted_iota(I32, (tq, width), 1)
                keep = jnp.concatenate([col <= row, col <= row], axis=0)
                s = jnp.where(keep, s, NEG_BIG)
            cols = [s[:, c * LANES:(c + 1) * LANES] for c in range(ncol)]
            part = cols[0]
            for c in range(1, ncol):
                part = jnp.maximum(part, cols[c])
            m_old = m_s[h]
            m_new = jnp.maximum(m_old, jnp.broadcast_to(jnp.max(part, axis=-1, keepdims=True),
                                                        (rows, LANES)))
            p = jnp.concatenate([jnp.exp((cb - m_new).astype(BF16)) for cb in cols], axis=1)
            v = v_ref[pl.ds(r0, width), h * dv:(h + 1) * dv]
            v_ext = jnp.concatenate([v, jnp.broadcast_to(ones, (width, dv))], axis=1)
            scale = jnp.exp(m_old - m_new)
            acc_s[h] = acc_s[h] * jnp.concatenate([scale, scale], axis=1) + _dot(p, v_ext)
            m_s[h] = m_new

    wide = 2 * tq

    n_wide = qi // 2

    def body(j, carry):
        block(pl.multiple_of(2 * j * wide, wide), wide, False)
        block(pl.multiple_of((2 * j + 1) * wide, wide), wide, False)
        return carry

    lax.fori_loop(0, n_wide // 2, body, 0)

    @pl.when(n_wide % 2 == 1)
    def _():
        block(pl.multiple_of((n_wide - 1) * wide, wide), wide, False)

    @pl.when(qi % 2 == 1)
    def _():
        block(pl.multiple_of((qi - 1) * tq, tq), tq, False)

    block(pl.multiple_of(qi * tq, tq), tq, True)

    lam = lam_ref[...]
    lam_val = (jnp.exp(jnp.sum(lam[0:1] * lam[1:2], axis=1, keepdims=True))
               - jnp.exp(jnp.sum(lam[2:3] * lam[3:4], axis=1, keepdims=True)) + lambda_init)
    for h in range(heads):
        acc = acc_s[h]
        o12 = acc[:, 0:dv] / acc[:, dv:2 * dv]
        o = o12[0:tq] - lam_val * o12[tq:rows]
        o = o * lax.rsqrt(jnp.mean(o * o, axis=-1, keepdims=True) + RMS_EPS) * sg_ref[...]
        o_ref[:, h * dv:(h + 1) * dv] = (o * (1.0 - lambda_init)).astype(BF16)


def _diff_attn(q, k, v, lam, subln_g, batch, seq, tq, lambda_init):
    t, d = q.shape
    nq = seq // tq
    heads = ATTN_HEADS_PER_STEP
    w = heads * DIFF_DV
    return pl.pallas_call(
        functools.partial(_diff_attn_kernel, tq=tq, lambda_init=lambda_init),
        grid=(batch, DIFF_HEADS // heads, nq),
        in_specs=[pl.BlockSpec((tq, w), lambda b, h, i: (b * nq + i, h)),
                  pl.BlockSpec((seq, w), lambda b, h, i: (b, h)),
                  pl.BlockSpec((seq, w), lambda b, h, i: (b, h)),
                  pl.BlockSpec(lam.shape, lambda b, h, i: (0, 0)),
                  pl.BlockSpec((1, DIFF_DV), lambda b, h, i: (0, 0))],
        out_specs=pl.BlockSpec((tq, w), lambda b, h, i: (b * nq + i, h)),
        out_shape=jax.ShapeDtypeStruct((t, d), BF16),
        scratch_shapes=[pltpu.VMEM((heads, 2 * tq, LANES), F32),
                        pltpu.VMEM((heads, 2 * tq, 2 * DIFF_DV), F32)],
        compiler_params=_cparams("parallel", "parallel", "arbitrary"),
        name="diff_attn",
    )(q, k, v, lam, subln_g)


def kernel(x, a_w_in, a_lb, a_norm_g, a_w_out, kv_w, b_w_q, b_lam, b_subln_g, b_w_out,
           ln1_g, ln1_b, ln2_g, ln2_b, router_w, router_b, moe_w_gate, moe_w_up, moe_w_down):
    batch, seq, d = x.shape
    t = batch * seq
    tm = min(512, t)
    xt = x.reshape(t, d)

    rw = jnp.pad(router_w, ((0, 0), (0, LANES - N_EXPERTS)))
    rw_hi = rw.astype(BF16)
    rw_cat = jnp.concatenate([rw_hi, (rw - rw_hi.astype(F32)).astype(BF16)], axis=1)
    rb_col = router_b.reshape(N_EXPERTS, 1)
    rw_rows = router_w.T.reshape(N_EXPERTS, 1, d)
    w_qkv = jnp.concatenate([b_w_q[0], kv_w], axis=1).astype(BF16)

    qkv = None
    for layer in range(DEPTH):
        if layer < N_A_LAYERS:
            q, k, v, g, lf = _hgrn_in(xt, a_w_in[layer].astype(BF16), a_lb, layer, tm)
            o = _hgrn_scan(q, k, v, lf, g, a_norm_g[layer], batch, seq, min(512, seq))
            w_out = a_w_out[layer]
        else:
            j = layer - N_A_LAYERS
            lambda_init = 0.8 - 0.6 * math.exp(-0.3 * layer)
            o = _diff_attn(*qkv, b_lam[j], b_subln_g[j].reshape(1, DIFF_DV), batch, seq,
                           min(ATTN_TQ, seq), lambda_init)
            w_out = b_w_out[j]
        x1, x1t, cls = _post_mix(xt, o, w_out.astype(BF16), ln1_g[layer].reshape(1, d),
                            ln1_b[layer].reshape(1, d), rw_cat, rb_col, tm)
        outs = _moe_layer(x1, x1t, cls, rw_rows, moe_w_gate[layer].astype(BF16),
                          moe_w_up[layer].astype(BF16), moe_w_down[layer].astype(BF16),
                          ln2_g[layer].reshape(1, d), ln2_b[layer].reshape(1, d),
                          w_qkv if layer + 1 == N_A_LAYERS else None)
        xt, qkv = outs[0], outs[1:]
    return xt.reshape(batch, seq, d)
```

```python
import functools
import math

import jax
import jax.numpy as jnp
from jax import lax
from jax.experimental import pallas as pl
from jax.experimental.pallas import tpu as pltpu

F32 = jnp.float32
BF16 = jnp.bfloat16
I32 = jnp.int32

D_MODEL = 1024
DEPTH = 2
N_A_LAYERS = DEPTH // 2
HGRN_HEADS = 8
HGRN_DH = D_MODEL // HGRN_HEADS
HGRN_CHUNK = 64
DIFF_HEADS = 8
DIFF_DH = D_MODEL // (2 * DIFF_HEADS)
DIFF_DV = 2 * DIFF_DH
N_EXPERTS = 16
N_GROUPS = 4
EXPERTS_PER_GROUP = N_EXPERTS // N_GROUPS
PAIRS_PER_GROUP = 6
N_CLASSES = N_GROUPS * PAIRS_PER_GROUP
CLASS_ROWS = 32
D_EXPERT = 512
DEEPNORM_ALPHA = (2.0 * DEPTH) ** 0.25
LN_EPS = 1e-5
RMS_EPS = 1e-6

LANES = 128
MOE_TILE = 256
LN2_TILE = 256
GATHER_SLOTS = 3
ATTN_TQ = 256
ATTN_HEADS_PER_STEP = 4
NEG_BIG = -1e30
VMEM_LIMIT = 56 * 1024 * 1024


def _cparams(*sem):
    return pltpu.CompilerParams(dimension_semantics=sem, vmem_limit_bytes=VMEM_LIMIT)


def _dot(a, b):
    return jnp.dot(a, b, preferred_element_type=F32)


def _dot_nt(a, b):
    return lax.dot_general(a, b, (((1,), (1,)), ((), ())), preferred_element_type=F32)


def _dot_tn(a, b):
    return lax.dot_general(a, b, (((0,), (0,)), ((), ())), preferred_element_type=F32)


def _split_bf16(x):
    hi = x.astype(BF16)
    lo = (x - hi.astype(F32)).astype(BF16)
    return hi, lo


def _layer_norm(z, g, b):
    mu = jnp.mean(z, axis=-1, keepdims=True)
    zc = z - mu
    var = jnp.mean(zc * zc, axis=-1, keepdims=True)
    return zc * lax.rsqrt(var + LN_EPS) * g + b


def _hgrn_in_kernel(x_ref, w_ref, lb_ref, q_ref, k_ref, v_ref, g_ref, lf_ref, *, layer):
    d = D_MODEL
    x = x_ref[...].astype(BF16)
    a = lb_ref[...]
    e = jnp.exp(a - jnp.max(a, axis=0, keepdims=True))
    sm = e / jnp.sum(e, axis=0, keepdims=True)
    lb = jnp.sum(sm[0:layer + 1], axis=0, keepdims=True)
    q_ref[...] = _dot(x, w_ref[:, 0:d]).astype(BF16)
    f = lb + (1.0 - lb) * jax.nn.sigmoid(_dot(x, w_ref[:, d:2 * d]))
    lf_ref[...] = jnp.log(f)
    k_ref[...] = (1.0 - f).astype(BF16)
    v_ref[...] = _dot(x, w_ref[:, 2 * d:3 * d]).astype(BF16)
    g_ref[...] = jax.nn.sigmoid(_dot(x, w_ref[:, 3 * d:4 * d])).astype(BF16)


def _hgrn_in(x, w_in_bf16, a_lb, layer, tm):
    t, d = x.shape
    row = pl.BlockSpec((tm, d), lambda i: (i, 0))
    return pl.pallas_call(
        functools.partial(_hgrn_in_kernel, layer=layer),
        grid=(t // tm,),
        in_specs=[row,
                  pl.BlockSpec((d, 4 * d), lambda i: (0, 0)),
                  pl.BlockSpec(a_lb.shape, lambda i: (0, 0))],
        out_specs=[row, row, row, row, row],
        out_shape=[jax.ShapeDtypeStruct((t, d), BF16)] * 4 + [jax.ShapeDtypeStruct((t, d), F32)],
        compiler_params=_cparams("parallel"),
        name="hgrn_in",
    )(x, w_in_bf16, a_lb)


def _hgrn_scan_kernel(q_ref, k_ref, v_ref, lf_ref, g_ref, ng_ref, o_ref, state_ref, *, n_chunks):
    c_len, dh = HGRN_CHUNK, HGRN_DH

    @pl.when(pl.program_id(1) == 0)
    def _():
        state_ref[...] = jnp.zeros_like(state_ref)

    row = lax.broadcasted_iota(I32, (c_len, c_len), 0)
    col = lax.broadcasted_iota(I32, (c_len, c_len), 1)
    causal = col <= row
    tri = jnp.where(causal, 1.0, 0.0).astype(BF16)

    heads = range(HGRN_HEADS)
    cols = [slice(h * dh, (h + 1) * dh) for h in heads]

    def chunk(c, carry):
        r0 = pl.multiple_of(c * c_len, c_len)
        rows = pl.ds(r0, c_len)
        lf_hi, lf_lo = _split_bf16(lf_ref[rows, :])
        b_all = _dot(tri, lf_hi) + _dot(tri, lf_lo)
        b = [b_all[:, cs] for cs in cols]
        q = [q_ref[rows, cs].astype(F32) for cs in cols]
        k = [k_ref[rows, cs].astype(F32) for cs in cols]
        v = [v_ref[rows, cs] for cs in cols]
        b_mid = [x[c_len // 2 - 1:c_len // 2, :] for x in b]
        b_last = [x[c_len - 1:c_len, :] for x in b]
        e_up = [jnp.exp(b[h] - b_mid[h]) for h in heads]
        e_dn = [jnp.exp(b_mid[h] - b[h]) for h in heads]
        qs = [q[h] * e_up[h] for h in heads]
        ks = [k[h] * e_dn[h] for h in heads]
        s = [jnp.where(causal, _dot_nt(qs[h].astype(BF16), ks[h].astype(BF16)), 0.0).astype(BF16)
             for h in heads]
        st = [state_ref[h] for h in heads]
        qd = [(qs[h] * jnp.exp(b_mid[h])).astype(BF16) for h in heads]
        o = [_dot(s[h], v[h]) + _dot_nt(qd[h], st[h].astype(BF16)) for h in heads]
        kd = [(ks[h] * jnp.exp(b_last[h] - b_mid[h])).astype(BF16) for h in heads]
        for h in heads:
            state_ref[h] = jnp.exp(b_last[h]) * st[h] + _dot_tn(v[h], kd[h])
        for h in heads:
            on = o[h] * lax.rsqrt(jnp.mean(o[h] * o[h], axis=-1, keepdims=True) + RMS_EPS)
            o_ref[rows, cols[h]] = (on * ng_ref[h:h + 1, :]
                                    * g_ref[rows, cols[h]].astype(F32)).astype(BF16)
        return carry

    lax.fori_loop(0, n_chunks, chunk, 0, unroll=2)


def _hgrn_scan(q, k, v, lf, g, norm_g, batch, seq, rows_per_step):
    t, d = q.shape
    steps = seq // rows_per_step
    blk = pl.BlockSpec((rows_per_step, d), lambda b, s: (b * steps + s, 0))
    return pl.pallas_call(
        functools.partial(_hgrn_scan_kernel, n_chunks=rows_per_step // HGRN_CHUNK),
        grid=(batch, steps),
        in_specs=[blk, blk, blk, blk, blk, pl.BlockSpec(norm_g.shape, lambda b, s: (0, 0))],
        out_specs=blk,
        out_shape=jax.ShapeDtypeStruct((t, d), BF16),
        scratch_shapes=[pltpu.VMEM((HGRN_HEADS, HGRN_DH, HGRN_DH), F32)],
        compiler_params=_cparams("parallel", "arbitrary"),
        name="hgrn_scan",
    )(q, k, v, lf, g, norm_g)


def _route_class(logits_t, rb):
    e = jnp.exp(logits_t - jnp.max(logits_t, axis=0, keepdims=True))
    sel = e / jnp.sum(e, axis=0, keepdims=True) + rb
    selr = [sel[i:i + 1, :] for i in range(N_EXPERTS)]
    n_per = EXPERTS_PER_GROUP

    def top2_sum(vals):
        best = None
        for i in range(len(vals)):
            for j in range(i + 1, len(vals)):
                pair = vals[i] + vals[j]
                best = pair if best is None else jnp.maximum(best, pair)
        return best

    gscore = [top2_sum(selr[g * n_per:(g + 1) * n_per]) for g in range(N_GROUPS)]
    gbest = jnp.zeros_like(gscore[0], dtype=I32)
    gval = gscore[0]
    for g in range(1, N_GROUPS):
        better = gscore[g] > gval
        gbest = jnp.where(better, g, gbest)
        gval = jnp.where(better, gscore[g], gval)

    def in_best_group(i):
        out = selr[(N_GROUPS - 1) * n_per + i]
        for g in range(N_GROUPS - 2, -1, -1):
            out = jnp.where(gbest == g, selr[g * n_per + i], out)
        return out

    sv = [in_best_group(i) for i in range(n_per)]
    i1 = jnp.zeros_like(gbest)
    v1 = sv[0]
    for i in range(1, n_per):
        better = sv[i] > v1
        i1 = jnp.where(better, i, i1)
        v1 = jnp.where(better, sv[i], v1)
    i2 = jnp.full_like(gbest, -1)
    v2 = jnp.full_like(v1, -jnp.inf)
    for i in range(n_per):
        better = (i1 != i) & (sv[i] > v2)
        i2 = jnp.where(better, i, i2)
        v2 = jnp.where(better, sv[i], v2)
    lo = jnp.minimum(i1, i2)
    hi = jnp.maximum(i1, i2)
    pair = jnp.where(lo == 0, hi - 1, jnp.where(lo == 1, hi + 1, PAIRS_PER_GROUP - 1))
    return gbest * PAIRS_PER_GROUP + pair


def _post_mix_kernel(x_ref, o_ref, w_ref, g_ref, b_ref, rw_ref, rb_ref, y_ref, yt_ref, cls_ref,
                     *, sub):
    nchunk = D_MODEL // LANES
    for r0 in range(0, x_ref.shape[0], sub):
        rows = slice(r0, r0 + sub)
        z = DEEPNORM_ALPHA * x_ref[rows, :] + _dot(o_ref[rows, :], w_ref[...])
        y = _layer_norm(z, g_ref[...], b_ref[...])
        y_ref[rows, :] = y
        for c in range(nchunk):
            yt_ref[pl.ds(r0 * nchunk + c, sub, stride=nchunk), :] = y[:, c * LANES:(c + 1) * LANES]
        yh, yl = _split_bf16(y)
        both = _dot(yh, rw_ref[...])
        logits = both[:, 0:LANES] + both[:, LANES:2 * LANES] + _dot(yl, rw_ref[:, 0:LANES])
        cls_ref[:, rows] = _route_class(jnp.transpose(logits)[0:N_EXPERTS, :], rb_ref[...])


def _post_mix(x, o, w_out_bf16, ln_g, ln_b, rw_cat, rb_col, tm):
    t, d = o.shape
    row = pl.BlockSpec((tm, d), lambda i: (i, 0))
    vec = pl.BlockSpec((1, d), lambda i: (0, 0))
    return pl.pallas_call(
        functools.partial(_post_mix_kernel, sub=min(256, tm)),
        grid=(t // tm,),
        in_specs=[row, row,
                  pl.BlockSpec((d, d), lambda i: (0, 0)),
                  vec, vec,
                  pl.BlockSpec((d, 2 * LANES), lambda i: (0, 0)),
                  pl.BlockSpec((N_EXPERTS, 1), lambda i: (0, 0))],
        out_specs=[row, pl.BlockSpec((tm * (d // LANES), LANES), lambda i: (i, 0)),
                   pl.BlockSpec((1, tm), lambda i: (0, i))],
        out_shape=[jax.ShapeDtypeStruct((t, d), F32),
                   jax.ShapeDtypeStruct((t * (d // LANES), LANES), F32),
                   jax.ShapeDtypeStruct((1, t), I32)],
        compiler_params=_cparams("parallel"),
        name="post_mix",
    )(x, o, w_out_bf16, ln_g, ln_b, rw_cat, rb_col)


def _plan_kernel(cls_ref, pos_ref, cnt_ref, cnt_s, base_s, run_s, *, tile):
    phase, i = pl.program_id(0), pl.program_id(1)
    tb = cls_ref.shape[1]
    onehot = cls_ref[...] == lax.broadcasted_iota(I32, (CLASS_ROWS, tb), 0)
    ohf = jnp.where(onehot, 1.0, 0.0)
    block_cnt = jnp.sum(ohf, axis=1, keepdims=True)

    @pl.when((phase == 0) & (i == 0))
    def _():
        cnt_s[...] = jnp.zeros_like(cnt_s)

    @pl.when(phase == 0)
    def _():
        cnt_s[...] += block_cnt

    @pl.when((phase == 1) & (i == 0))
    def _():
        cnt = cnt_s[...]
        ntile = jnp.floor((cnt + (tile - 1)) * (1.0 / tile))
        nt_hi = jnp.floor(ntile * (1.0 / 256.0))
        nt_lo = ntile - 256.0 * nt_hi
        r = lax.broadcasted_iota(I32, (CLASS_ROWS, CLASS_ROWS), 0)
        c = lax.broadcasted_iota(I32, (CLASS_ROWS, CLASS_ROWS), 1)
        below = jnp.where(c < r, 1.0, 0.0).astype(BF16)
        wide = (CLASS_ROWS, LANES)
        excl = (256.0 * _dot(below, jnp.broadcast_to(nt_hi, wide).astype(BF16))
                + _dot(below, jnp.broadcast_to(nt_lo, wide).astype(BF16)))
        base_s[...] = excl[:, 0:1] * tile
        run_s[...] = jnp.zeros_like(run_s)
        cnt_ref[...] = cnt

    @pl.when(phase == 1)
    def _():
        r = lax.broadcasted_iota(I32, (tb, tb), 0)
        c = lax.broadcasted_iota(I32, (tb, tb), 1)
        upto = jnp.where(r <= c, 1.0, 0.0).astype(BF16)
        within = _dot(ohf.astype(BF16), upto)
        rank = within + (run_s[...] + base_s[...] - 1.0)
        pos_ref[...] = jnp.sum(jnp.where(onehot, rank, 0.0), axis=0, keepdims=True).astype(I32)
        run_s[...] += block_cnt


def _plan(cls, tb, tile):
    t = cls.shape[1]
    return pl.pallas_call(
        functools.partial(_plan_kernel, tile=tile),
        grid=(2, t // tb),
        in_specs=[pl.BlockSpec((1, tb), lambda p, i: (0, i))],
        out_specs=[pl.BlockSpec((1, tb), lambda p, i: (0, i * p)),
                   pl.BlockSpec((CLASS_ROWS, 1), lambda p, i: (0, 0))],
        out_shape=[jax.ShapeDtypeStruct((1, t), I32), jax.ShapeDtypeStruct((CLASS_ROWS, 1), F32)],
        scratch_shapes=[pltpu.VMEM((CLASS_ROWS, 1), F32)] * 3,
        compiler_params=_cparams("arbitrary", "arbitrary"),
        name="moe_plan",
    )(cls)


def _invert_kernel(pos_ref, pad_lo_ref, pad_hi_ref, src_ref):
    def clear(i, carry):
        src_ref[i] = 0
        return carry

    def place(t, carry):
        src_ref[pos_ref[t]] = t
        return carry

    for r in range(N_CLASSES + 1):
        lax.fori_loop(pad_lo_ref[r], pad_hi_ref[r], clear, 0)
    lax.fori_loop(0, pos_ref.shape[0], place, 0, unroll=8)


def _invert(pos, pad_lo, pad_hi, n_sorted):
    return pl.pallas_call(
        _invert_kernel,
        grid_spec=pltpu.PrefetchScalarGridSpec(
            num_scalar_prefetch=3, grid=(1,), in_specs=[],
            out_specs=pl.BlockSpec(memory_space=pltpu.SMEM)),
        out_shape=jax.ShapeDtypeStruct((n_sorted,), I32),
        compiler_params=_cparams("arbitrary"),
        name="moe_invert",
    )(pos, pad_lo, pad_hi)


def _start_row_gather(hbm, idx_ref, base, buf, slot, sem, lo, hi, rows=1):
    for j in range(lo, hi):
        src0 = idx_ref[base + j] * rows
        if rows > 1:
            src0 = pl.multiple_of(src0, rows)
        pltpu.make_async_copy(hbm.at[pl.ds(src0, rows)],
                              buf.at[slot, pl.ds(j * rows, rows)], sem).start()


def _row_groups(n, parts):
    edges = [n * p // parts for p in range(parts + 1)]
    return list(zip(edges[:-1], edges[1:]))


def _wait_row_gather(hbm, buf, slot, sem, n, rows=1):
    for j in range(n):
        pltpu.make_async_copy(hbm.at[pl.ds(0, rows)],
                              buf.at[slot, pl.ds(j * rows, rows)], sem).wait()


def _experts_kernel(ea_ref, eb_ref, nvalid_ref, src_ref, x_hbm, rwa_ref, rwb_ref,
                    wga_ref, wua_ref, wda_ref, wgb_ref, wub_ref, wdb_ref, ys_ref,
                    xbuf, xb_s, h_s, sems, *, tile):
    i = pl.program_id(0)
    nvalid = nvalid_ref[0]
    last = nvalid - 1
    nchunk = D_MODEL // LANES

    @pl.when(i == 0)
    def _():
        _start_row_gather(x_hbm, src_ref, 0, xbuf, 0, sems.at[0], 0, tile, nchunk)
        _start_row_gather(x_hbm, src_ref, jnp.minimum(1, last) * tile, xbuf, 1, sems.at[1],
                          0, tile, nchunk)

    def run(slot):
        _wait_row_gather(x_hbm, xbuf, slot, sems.at[slot], tile, nchunk)
        base = jnp.minimum(i + 2, last) * tile
        ahead = (slot + 2) % GATHER_SLOTS
        groups = _row_groups(tile, 4)

        def prefetch(p):
            _start_row_gather(x_hbm, src_ref, base, xbuf, ahead, sems.at[ahead], *groups[p], nchunk)

        xcur = xbuf.at[slot]
        xf = jnp.concatenate([xcur[pl.ds(c, tile, stride=nchunk), :] for c in range(nchunk)], axis=1)
        dlogit = jnp.sum(xf * (rwa_ref[0] - rwb_ref[0]), axis=-1, keepdims=True)
        g_lo = 1.0 / (1.0 + jnp.exp(-dlogit))
        g_hi = 1.0 / (1.0 + jnp.exp(dlogit))
        xb_s[...] = xf.astype(BF16)
        prefetch(0)
        x = xb_s[...]
        h_s[:, 0:D_EXPERT] = (jax.nn.silu(_dot(x, wga_ref[0])) * _dot(x, wua_ref[0])
                              * g_lo).astype(BF16)
        prefetch(1)
        h_s[:, D_EXPERT:2 * D_EXPERT] = (jax.nn.silu(_dot(x, wgb_ref[0])) * _dot(x, wub_ref[0])
                                         * g_hi).astype(BF16)
        prefetch(2)
        ys_ref[...] = (_dot(h_s[:, 0:D_EXPERT], wda_ref[0])
                       + _dot(h_s[:, D_EXPERT:2 * D_EXPERT], wdb_ref[0]))
        prefetch(3)

    for s in range(GATHER_SLOTS):
        pl.when((i < nvalid) & (i % GATHER_SLOTS == s))(functools.partial(run, s))

    @pl.when(i == last)
    def _():
        for ahead in (1, 2):
            slot = (i + ahead) % GATHER_SLOTS
            _wait_row_gather(x_hbm, xbuf, slot, sems.at[slot], tile, nchunk)

    @pl.when(i >= nvalid)
    def _():
        ys_ref[...] = jnp.zeros_like(ys_ref)


def _experts(ea, eb, nvalid, src, x, rw_rows, w_gate, w_up, w_down, tile):
    n_sorted = src.shape[0]
    d, f = D_MODEL, D_EXPERT
    rw_a = pl.BlockSpec((1, 1, d), lambda i, ea, eb, nv, src: (ea[i], 0, 0))
    rw_b = pl.BlockSpec((1, 1, d), lambda i, ea, eb, nv, src: (eb[i], 0, 0))
    up_a = pl.BlockSpec((1, d, f), lambda i, ea, eb, nv, src: (ea[i], 0, 0))
    up_b = pl.BlockSpec((1, d, f), lambda i, ea, eb, nv, src: (eb[i], 0, 0))
    dn_a = pl.BlockSpec((1, f, d), lambda i, ea, eb, nv, src: (ea[i], 0, 0))
    dn_b = pl.BlockSpec((1, f, d), lambda i, ea, eb, nv, src: (eb[i], 0, 0))
    return pl.pallas_call(
        functools.partial(_experts_kernel, tile=tile),
        grid_spec=pltpu.PrefetchScalarGridSpec(
            num_scalar_prefetch=4, grid=(n_sorted // tile,),
            in_specs=[pl.BlockSpec(memory_space=pl.ANY), rw_a, rw_b,
                      up_a, up_a, dn_a, up_b, up_b, dn_b],
            out_specs=pl.BlockSpec((tile, d), lambda i, ea, eb, nv, src: (i, 0)),
            scratch_shapes=[pltpu.VMEM((GATHER_SLOTS, tile * (d // LANES), LANES), F32),
                            pltpu.VMEM((tile, d), BF16),
                            pltpu.VMEM((tile, 2 * f), BF16),
                            pltpu.SemaphoreType.DMA((GATHER_SLOTS,))]),
        out_shape=jax.ShapeDtypeStruct((n_sorted, d), F32),
        compiler_params=_cparams("arbitrary"),
        name="moe_experts",
    )(ea, eb, nvalid, src, x, rw_rows, rw_rows, w_gate, w_up, w_down, w_gate, w_up, w_down)


def _ln2_kernel(pos_ref, x_ref, ys_hbm, g_ref, b_ref, *rest, tile, with_qkv):
    if with_qkv:
        w_ref, o_ref, q_ref, k_ref, v_ref, ybuf, sems = rest
    else:
        o_ref, ybuf, sems = rest
    i = pl.program_id(0)
    last = pl.num_programs(0) - 1

    @pl.when(i == 0)
    def _():
        _start_row_gather(ys_hbm, pos_ref, 0, ybuf, 0, sems.at[0], 0, tile)
        _start_row_gather(ys_hbm, pos_ref, jnp.minimum(1, last) * tile, ybuf, 1, sems.at[1], 0, tile)

    def run(slot):
        _wait_row_gather(ys_hbm, ybuf, slot, sems.at[slot], tile)
        base = jnp.minimum(i + 2, last) * tile
        ahead = (slot + 2) % GATHER_SLOTS
        groups = _row_groups(tile, 4 if with_qkv else 1)

        def prefetch(p):
            _start_row_gather(ys_hbm, pos_ref, base, ybuf, ahead, sems.at[ahead], *groups[p])

        z = DEEPNORM_ALPHA * x_ref[...] + ybuf[slot]
        prefetch(0)
        y = _layer_norm(z, g_ref[...], b_ref[...])
        o_ref[...] = y
        if with_qkv:
            d = D_MODEL
            yb = y.astype(BF16)
            q_ref[...] = (_dot(yb, w_ref[:, 0:d]) * (DIFF_DH ** -0.5)).astype(BF16)
            prefetch(1)
            k_ref[...] = _dot(yb, w_ref[:, d:2 * d]).astype(BF16)
            prefetch(2)
            v_ref[...] = _dot(yb, w_ref[:, 2 * d:3 * d]).astype(BF16)
            prefetch(3)

    for s in range(GATHER_SLOTS):
        pl.when(i % GATHER_SLOTS == s)(functools.partial(run, s))

    @pl.when(i == last)
    def _():
        for ahead in (1, 2):
            slot = (i + ahead) % GATHER_SLOTS
            _wait_row_gather(ys_hbm, ybuf, slot, sems.at[slot], tile)


def _ln2(pos, x1, ys, ln_g, ln_b, w_qkv_bf16, tile):
    t, d = x1.shape
    with_qkv = w_qkv_bf16 is not None
    row = pl.BlockSpec((tile, d), lambda i, pos: (i, 0))
    vec = pl.BlockSpec((1, d), lambda i, pos: (0, 0))
    in_specs = [row, pl.BlockSpec(memory_space=pl.ANY), vec, vec]
    out_specs = [row]
    out_shape = [jax.ShapeDtypeStruct((t, d), F32)]
    args = [pos, x1, ys, ln_g, ln_b]
    if with_qkv:
        in_specs.append(pl.BlockSpec((d, 3 * d), lambda i, pos: (0, 0)))
        out_specs += [row, row, row]
        out_shape += [jax.ShapeDtypeStruct((t, d), BF16)] * 3
        args.append(w_qkv_bf16)
    return pl.pallas_call(
        functools.partial(_ln2_kernel, tile=tile, with_qkv=with_qkv),
        grid_spec=pltpu.PrefetchScalarGridSpec(
            num_scalar_prefetch=1, grid=(t // tile,),
            in_specs=in_specs, out_specs=out_specs,
            scratch_shapes=[pltpu.VMEM((GATHER_SLOTS, tile, d), F32), pltpu.SemaphoreType.DMA((GATHER_SLOTS,))]),
        out_shape=out_shape,
        compiler_params=_cparams("arbitrary"),
        name="ln2_qkv" if with_qkv else "ln2",
    )(*args)


def _moe_layer(x1, x1t, cls, rw_rows, w_gate, w_up, w_down, ln_g, ln_b, w_qkv_bf16):
    t = x1.shape[0]
    tile = min(MOE_TILE, t)
    n_tiles = t // tile + N_CLASSES
    pos, cnt = _plan(cls, min(512, t), tile)
    pos = pos.reshape(t)
    cnt = cnt[:N_CLASSES, 0].astype(I32)
    ntile = (cnt + tile - 1) // tile
    tile_end = jnp.cumsum(ntile)
    nvalid = tile_end[-1]
    tid = jnp.minimum(jnp.arange(n_tiles, dtype=I32), nvalid - 1)
    tcls = jnp.sum((tid[:, None] >= tile_end[None, :]).astype(I32), axis=1)
    grp, pair = tcls // PAIRS_PER_GROUP, tcls % PAIRS_PER_GROUP
    ea = grp * EXPERTS_PER_GROUP + jnp.array([0, 0, 0, 1, 1, 2], I32)[pair]
    eb = grp * EXPERTS_PER_GROUP + jnp.array([1, 2, 3, 2, 3, 3], I32)[pair]
    base = (tile_end - ntile) * tile
    n_sorted = n_tiles * tile
    pad_lo = jnp.concatenate([base + cnt, (nvalid * tile).reshape(1)])
    pad_hi = jnp.concatenate([base + ntile * tile, jnp.full((1,), n_sorted, I32)])
    src = _invert(pos, pad_lo, pad_hi, n_sorted)
    ys = _experts(ea, eb, nvalid.reshape(1), src, x1t, rw_rows, w_gate, w_up, w_down, tile)
    return _ln2(pos, x1, ys, ln_g, ln_b, w_qkv_bf16, min(LN2_TILE, t))


def _diff_attn_kernel(q_ref, k_ref, v_ref, lam_ref, sg_ref, o_ref, m_s, acc_s, *, tq, lambda_init):
    qi = pl.program_id(2)
    dh, dv = DIFF_DH, DIFF_DV
    heads = ATTN_HEADS_PER_STEP
    rows = 2 * tq
    lane = lax.broadcasted_iota(I32, (tq, dv), 1)
    qs = []
    for h in range(heads):
        q = q_ref[:, h * dv:(h + 1) * dv]
        zero = jnp.zeros_like(q)
        qs.append(jnp.concatenate([jnp.where(lane < dh, q, zero), jnp.where(lane >= dh, q, zero)],
                                  axis=0))
    m_s[...] = jnp.full_like(m_s, NEG_BIG)
    acc_s[...] = jnp.zeros_like(acc_s)
    ones = jnp.ones((1, dv), BF16)

    def block(r0, width, masked):
        ncol = width // LANES
        for h in range(heads):
            s = _dot_nt(qs[h], k_ref[pl.ds(r0, width), h * dv:(h + 1) * dv])
            if masked:
                row = lax.broadcasted_iota(I32, (tq, width), 0)
                col = lax.broadcasted_iota(I32, (tq, width), 1)
                keep = jnp.concatenate([col <= row, col <= row], axis=0)
                s = jnp.where(keep, s, NEG_BIG)
            cols = [s[:, c * LANES:(c + 1) * LANES] for c in range(ncol)]
            part = cols[0]
            for c in range(1, ncol):
                part = jnp.maximum(part, cols[c])
            m_old = m_s[h]
            m_new = jnp.maximum(m_old, jnp.broadcast_to(jnp.max(part, axis=-1, keepdims=True),
                                                        (rows, LANES)))
            p = jnp.concatenate([jnp.exp((cb - m_new).astype(BF16)) for cb in cols], axis=1)
            v = v_ref[pl.ds(r0, width), h * dv:(h + 1) * dv]
            v_ext = jnp.concatenate([v, jnp.broadcast_to(ones, (width, dv))], axis=1)
            scale = jnp.exp(m_old - m_new)
            acc_s[h] = acc_s[h] * jnp.concatenate([scale, scale], axis=1) + _dot(p, v_ext)
            m_s[h] = m_new

    wide = 2 * tq

    n_wide = qi // 2

    def body(j, carry):
        block(pl.multiple_of(2 * j * wide, wide), wide, False)
        block(pl.multiple_of((2 * j + 1) * wide, wide), wide, False)
        return carry

    lax.fori_loop(0, n_wide // 2, body, 0)

    @pl.when(n_wide % 2 == 1)
    def _():
        block(pl.multiple_of((n_wide - 1) * wide, wide), wide, False)

    @pl.when(qi % 2 == 1)
    def _():
        block(pl.multiple_of((qi - 1) * tq, tq), tq, False)

    block(pl.multiple_of(qi * tq, tq), tq, True)

    lam = lam_ref[...]
    lam_val = (jnp.exp(jnp.sum(lam[0:1] * lam[1:2], axis=1, keepdims=True))
               - jnp.exp(jnp.sum(lam[2:3] * lam[3:4], axis=1, keepdims=True)) + lambda_init)
    for h in range(heads):
        acc = acc_s[h]
        o12 = acc[:, 0:dv] / acc[:, dv:2 * dv]
        o = o12[0:tq] - lam_val * o12[tq:rows]
        o = o * lax.rsqrt(jnp.mean(o * o, axis=-1, keepdims=True) + RMS_EPS) * sg_ref[...]
        o_ref[:, h * dv:(h + 1) * dv] = (o * (1.0 - lambda_init)).astype(BF16)


def _diff_attn(q, k, v, lam, subln_g, batch, seq, tq, lambda_init):
    t, d = q.shape
    nq = seq // tq
    heads = ATTN_HEADS_PER_STEP
    w = heads * DIFF_DV
    return pl.pallas_call(
        functools.partial(_diff_attn_kernel, tq=tq, lambda_init=lambda_init),
        grid=(batch, DIFF_HEADS // heads, nq),
        in_specs=[pl.BlockSpec((tq, w), lambda b, h, i: (b * nq + i, h)),
                  pl.BlockSpec((seq, w), lambda b, h, i: (b, h)),
                  pl.BlockSpec((seq, w), lambda b, h, i: (b, h)),
                  pl.BlockSpec(lam.shape, lambda b, h, i: (0, 0)),
                  pl.BlockSpec((1, DIFF_DV), lambda b, h, i: (0, 0))],
        out_specs=pl.BlockSpec((tq, w), lambda b, h, i: (b * nq + i, h)),
        out_shape=jax.ShapeDtypeStruct((t, d), BF16),
        scratch_shapes=[pltpu.VMEM((heads, 2 * tq, LANES), F32),
                        pltpu.VMEM((heads, 2 * tq, 2 * DIFF_DV), F32)],
        compiler_params=_cparams("parallel", "parallel", "arbitrary"),
        name="diff_attn",
    )(q, k, v, lam, subln_g)


def kernel(x, a_w_in, a_lb, a_norm_g, a_w_out, kv_w, b_w_q, b_lam, b_subln_g, b_w_out,
           ln1_g, ln1_b, ln2_g, ln2_b, router_w, router_b, moe_w_gate, moe_w_up, moe_w_down):
    batch, seq, d = x.shape
    t = batch * seq
    tm = min(512, t)
    xt = x.reshape(t, d)

    rw = jnp.pad(router_w, ((0, 0), (0, LANES - N_EXPERTS)))
    rw_hi = rw.astype(BF16)
    rw_cat = jnp.concatenate([rw_hi, (rw - rw_hi.astype(F32)).astype(BF16)], axis=1)
    rb_col = router_b.reshape(N_EXPERTS, 1)
    rw_rows = router_w.T.reshape(N_EXPERTS, 1, d)
    w_qkv = jnp.concatenate([b_w_q[0], kv_w], axis=1).astype(BF16)

    qkv = None
    for layer in range(DEPTH):
        if layer < N_A_LAYERS:
            q, k, v, g, lf = _hgrn_in(xt, a_w_in[layer].astype(BF16), a_lb, layer, tm)
            o = _hgrn_scan(q, k, v, lf, g, a_norm_g[layer], batch, seq, min(512, seq))
            w_out = a_w_out[layer]
        else:
            j = layer - N_A_LAYERS
            lambda_init = 0.8 - 0.6 * math.exp(-0.3 * layer)
            o = _diff_attn(*qkv, b_lam[j], b_subln_g[j].reshape(1, DIFF_DV), batch, seq,
                           min(ATTN_TQ, seq), lambda_init)
            w_out = b_w_out[j]
        x1, x1t, cls = _post_mix(xt, o, w_out.astype(BF16), ln1_g[layer].reshape(1, d),
                            ln1_b[layer].reshape(1, d), rw_cat, rb_col, tm)
        outs = _moe_layer(x1, x1t, cls, rw_rows, moe_w_gate[layer].astype(BF16),
                          moe_w_up[layer].astype(BF16), moe_w_down[layer].astype(BF16),
                          ln2_g[layer].reshape(1, d), ln2_b[layer].reshape(1, d),
                          w_qkv if layer + 1 == N_A_LAYERS else None)
        xt, qkv = outs[0], outs[1:]
    return xt.reshape(batch, seq, d)
```

```python
import functools
import math

import jax
import jax.numpy as jnp
from jax import lax
from jax.experimental import pallas as pl
from jax.experimental.pallas import tpu as pltpu

F32 = jnp.float32
BF16 = jnp.bfloat16
I32 = jnp.int32

D_MODEL = 1024
DEPTH = 2
N_A_LAYERS = DEPTH // 2
HGRN_HEADS = 8
HGRN_DH = D_MODEL // HGRN_HEADS
HGRN_CHUNK = 64
DIFF_HEADS = 8
DIFF_DH = D_MODEL // (2 * DIFF_HEADS)
DIFF_DV = 2 * DIFF_DH
N_EXPERTS = 16
N_GROUPS = 4
EXPERTS_PER_GROUP = N_EXPERTS // N_GROUPS
PAIRS_PER_GROUP = 6
N_CLASSES = N_GROUPS * PAIRS_PER_GROUP
CLASS_ROWS = 32
D_EXPERT = 512
DEEPNORM_ALPHA = (2.0 * DEPTH) ** 0.25
LN_EPS = 1e-5
RMS_EPS = 1e-6

LANES = 128
MOE_TILE = 256
LN2_TILE = 256
GATHER_SLOTS = 3
ATTN_TQ = 256
ATTN_HEADS_PER_STEP = 8
NEG_BIG = -1e30
VMEM_LIMIT = 56 * 1024 * 1024


def _cparams(*sem):
    return pltpu.CompilerParams(dimension_semantics=sem, vmem_limit_bytes=VMEM_LIMIT)


def _dot(a, b):
    return jnp.dot(a, b, preferred_element_type=F32)


def _dot_nt(a, b):
    return lax.dot_general(a, b, (((1,), (1,)), ((), ())), preferred_element_type=F32)


def _dot_tn(a, b):
    return lax.dot_general(a, b, (((0,), (0,)), ((), ())), preferred_element_type=F32)


def _split_bf16(x):
    hi = x.astype(BF16)
    lo = (x - hi.astype(F32)).astype(BF16)
    return hi, lo


def _layer_norm(z, g, b):
    mu = jnp.mean(z, axis=-1, keepdims=True)
    zc = z - mu
    var = jnp.mean(zc * zc, axis=-1, keepdims=True)
    return zc * lax.rsqrt(var + LN_EPS) * g + b


def _hgrn_in_kernel(x_ref, w_ref, lb_ref, q_ref, k_ref, v_ref, g_ref, lf_ref, *, layer):
    d = D_MODEL
    x = x_ref[...].astype(BF16)
    a = lb_ref[...]
    e = jnp.exp(a - jnp.max(a, axis=0, keepdims=True))
    sm = e / jnp.sum(e, axis=0, keepdims=True)
    lb = jnp.sum(sm[0:layer + 1], axis=0, keepdims=True)
    q_ref[...] = _dot(x, w_ref[:, 0:d]).astype(BF16)
    f = lb + (1.0 - lb) * jax.nn.sigmoid(_dot(x, w_ref[:, d:2 * d]))
    lf_ref[...] = jnp.log(f)
    k_ref[...] = (1.0 - f).astype(BF16)
    v_ref[...] = _dot(x, w_ref[:, 2 * d:3 * d]).astype(BF16)
    g_ref[...] = jax.nn.sigmoid(_dot(x, w_ref[:, 3 * d:4 * d])).astype(BF16)


def _hgrn_in(x, w_in_bf16, a_lb, layer, tm):
    t, d = x.shape
    row = pl.BlockSpec((tm, d), lambda i: (i, 0))
    return pl.pallas_call(
        functools.partial(_hgrn_in_kernel, layer=layer),
        grid=(t // tm,),
        in_specs=[row,
                  pl.BlockSpec((d, 4 * d), lambda i: (0, 0)),
                  pl.BlockSpec(a_lb.shape, lambda i: (0, 0))],
        out_specs=[row, row, row, row, row],
        out_shape=[jax.ShapeDtypeStruct((t, d), BF16)] * 4 + [jax.ShapeDtypeStruct((t, d), F32)],
        compiler_params=_cparams("parallel"),
        name="hgrn_in",
    )(x, w_in_bf16, a_lb)


def _hgrn_scan_kernel(q_ref, k_ref, v_ref, lf_ref, g_ref, ng_ref, o_ref, state_ref, *, n_chunks):
    c_len, dh = HGRN_CHUNK, HGRN_DH

    @pl.when(pl.program_id(1) == 0)
    def _():
        state_ref[...] = jnp.zeros_like(state_ref)

    row = lax.broadcasted_iota(I32, (c_len, c_len), 0)
    col = lax.broadcasted_iota(I32, (c_len, c_len), 1)
    causal = col <= row
    tri = jnp.where(causal, 1.0, 0.0).astype(BF16)

    heads = range(HGRN_HEADS)
    cols = [slice(h * dh, (h + 1) * dh) for h in heads]

    def chunk(c, carry):
        r0 = pl.multiple_of(c * c_len, c_len)
        rows = pl.ds(r0, c_len)
        lf_hi, lf_lo = _split_bf16(lf_ref[rows, :])
        b_all = _dot(tri, lf_hi) + _dot(tri, lf_lo)
        b = [b_all[:, cs] for cs in cols]
        q = [q_ref[rows, cs].astype(F32) for cs in cols]
        k = [k_ref[rows, cs].astype(F32) for cs in cols]
        v = [v_ref[rows, cs] for cs in cols]
        b_mid = [x[c_len // 2 - 1:c_len // 2, :] for x in b]
        b_last = [x[c_len - 1:c_len, :] for x in b]
        e_up = [jnp.exp(b[h] - b_mid[h]) for h in heads]
        e_dn = [jnp.exp(b_mid[h] - b[h]) for h in heads]
        qs = [q[h] * e_up[h] for h in heads]
        ks = [k[h] * e_dn[h] for h in heads]
        s = [jnp.where(causal, _dot_nt(qs[h].astype(BF16), ks[h].astype(BF16)), 0.0).astype(BF16)
             for h in heads]
        st = [state_ref[h] for h in heads]
        qd = [(qs[h] * jnp.exp(b_mid[h])).astype(BF16) for h in heads]
        o = [_dot(s[h], v[h]) + _dot_nt(qd[h], st[h].astype(BF16)) for h in heads]
        kd = [(ks[h] * jnp.exp(b_last[h] - b_mid[h])).astype(BF16) for h in heads]
        for h in heads:
            state_ref[h] = jnp.exp(b_last[h]) * st[h] + _dot_tn(v[h], kd[h])
        for h in heads:
            on = o[h] * lax.rsqrt(jnp.mean(o[h] * o[h], axis=-1, keepdims=True) + RMS_EPS)
            o_ref[rows, cols[h]] = (on * ng_ref[h:h + 1, :]
                                    * g_ref[rows, cols[h]].astype(F32)).astype(BF16)
        return carry

    lax.fori_loop(0, n_chunks, chunk, 0, unroll=2)


def _hgrn_scan(q, k, v, lf, g, norm_g, batch, seq, rows_per_step):
    t, d = q.shape
    steps = seq // rows_per_step
    blk = pl.BlockSpec((rows_per_step, d), lambda b, s: (b * steps + s, 0))
    return pl.pallas_call(
        functools.partial(_hgrn_scan_kernel, n_chunks=rows_per_step // HGRN_CHUNK),
        grid=(batch, steps),
        in_specs=[blk, blk, blk, blk, blk, pl.BlockSpec(norm_g.shape, lambda b, s: (0, 0))],
        out_specs=blk,
        out_shape=jax.ShapeDtypeStruct((t, d), BF16),
        scratch_shapes=[pltpu.VMEM((HGRN_HEADS, HGRN_DH, HGRN_DH), F32)],
        compiler_params=_cparams("parallel", "arbitrary"),
        name="hgrn_scan",
    )(q, k, v, lf, g, norm_g)


def _route_class(logits_t, rb):
    e = jnp.exp(logits_t - jnp.max(logits_t, axis=0, keepdims=True))
    sel = e / jnp.sum(e, axis=0, keepdims=True) + rb
    selr = [sel[i:i + 1, :] for i in range(N_EXPERTS)]
    n_per = EXPERTS_PER_GROUP

    def top2_sum(vals):
        best = None
        for i in range(len(vals)):
            for j in range(i + 1, len(vals)):
                pair = vals[i] + vals[j]
                best = pair if best is None else jnp.maximum(best, pair)
        return best

    gscore = [top2_sum(selr[g * n_per:(g + 1) * n_per]) for g in range(N_GROUPS)]
    gbest = jnp.zeros_like(gscore[0], dtype=I32)
    gval = gscore[0]
    for g in range(1, N_GROUPS):
        better = gscore[g] > gval
        gbest = jnp.where(better, g, gbest)
        gval = jnp.where(better, gscore[g], gval)

    def in_best_group(i):
        out = selr[(N_GROUPS - 1) * n_per + i]
        for g in range(N_GROUPS - 2, -1, -1):
            out = jnp.where(gbest == g, selr[g * n_per + i], out)
        return out

    sv = [in_best_group(i) for i in range(n_per)]
    i1 = jnp.zeros_like(gbest)
    v1 = sv[0]
    for i in range(1, n_per):
        better = sv[i] > v1
        i1 = jnp.where(better, i, i1)
        v1 = jnp.where(better, sv[i], v1)
    i2 = jnp.full_like(gbest, -1)
    v2 = jnp.full_like(v1, -jnp.inf)
    for i in range(n_per):
        better = (i1 != i) & (sv[i] > v2)
        i2 = jnp.where(better, i, i2)
        v2 = jnp.where(better, sv[i], v2)
    lo = jnp.minimum(i1, i2)
    hi = jnp.maximum(i1, i2)
    pair = jnp.where(lo == 0, hi - 1, jnp.where(lo == 1, hi + 1, PAIRS_PER_GROUP - 1))
    return gbest * PAIRS_PER_GROUP + pair


def _post_mix_kernel(x_ref, o_ref, w_ref, g_ref, b_ref, rw_ref, rb_ref, y_ref, yt_ref, cls_ref,
                     *, sub):
    nchunk = D_MODEL // LANES
    for r0 in range(0, x_ref.shape[0], sub):
        rows = slice(r0, r0 + sub)
        z = DEEPNORM_ALPHA * x_ref[rows, :] + _dot(o_ref[rows, :], w_ref[...])
        y = _layer_norm(z, g_ref[...], b_ref[...])
        y_ref[rows, :] = y
        for c in range(nchunk):
            yt_ref[pl.ds(r0 * nchunk + c, sub, stride=nchunk), :] = y[:, c * LANES:(c + 1) * LANES]
        yh, yl = _split_bf16(y)
        both = _dot(yh, rw_ref[...])
        logits = both[:, 0:LANES] + both[:, LANES:2 * LANES] + _dot(yl, rw_ref[:, 0:LANES])
        cls_ref[:, rows] = _route_class(jnp.transpose(logits)[0:N_EXPERTS, :], rb_ref[...])


def _post_mix(x, o, w_out_bf16, ln_g, ln_b, rw_cat, rb_col, tm):
    t, d = o.shape
    row = pl.BlockSpec((tm, d), lambda i: (i, 0))
    vec = pl.BlockSpec((1, d), lambda i: (0, 0))
    return pl.pallas_call(
        functools.partial(_post_mix_kernel, sub=min(256, tm)),
        grid=(t // tm,),
        in_specs=[row, row,
                  pl.BlockSpec((d, d), lambda i: (0, 0)),
                  vec, vec,
                  pl.BlockSpec((d, 2 * LANES), lambda i: (0, 0)),
                  pl.BlockSpec((N_EXPERTS, 1), lambda i: (0, 0))],
        out_specs=[row, pl.BlockSpec((tm * (d // LANES), LANES), lambda i: (i, 0)),
                   pl.BlockSpec((1, tm), lambda i: (0, i))],
        out_shape=[jax.ShapeDtypeStruct((t, d), F32),
                   jax.ShapeDtypeStruct((t * (d // LANES), LANES), F32),
                   jax.ShapeDtypeStruct((1, t), I32)],
        compiler_params=_cparams("parallel"),
        name="post_mix",
    )(x, o, w_out_bf16, ln_g, ln_b, rw_cat, rb_col)


def _plan_kernel(cls_ref, pos_ref, cnt_ref, cnt_s, base_s, run_s, *, tile):
    phase, i = pl.program_id(0), pl.program_id(1)
    tb = cls_ref.shape[1]
    onehot = cls_ref[...] == lax.broadcasted_iota(I32, (CLASS_ROWS, tb), 0)
    ohf = jnp.where(onehot, 1.0, 0.0)
    block_cnt = jnp.sum(ohf, axis=1, keepdims=True)

    @pl.when((phase == 0) & (i == 0))
    def _():
        cnt_s[...] = jnp.zeros_like(cnt_s)

    @pl.when(phase == 0)
    def _():
        cnt_s[...] += block_cnt

    @pl.when((phase == 1) & (i == 0))
    def _():
        cnt = cnt_s[...]
        ntile = jnp.floor((cnt + (tile - 1)) * (1.0 / tile))
        nt_hi = jnp.floor(ntile * (1.0 / 256.0))
        nt_lo = ntile - 256.0 * nt_hi
        r = lax.broadcasted_iota(I32, (CLASS_ROWS, CLASS_ROWS), 0)
        c = lax.broadcasted_iota(I32, (CLASS_ROWS, CLASS_ROWS), 1)
        below = jnp.where(c < r, 1.0, 0.0).astype(BF16)
        wide = (CLASS_ROWS, LANES)
        excl = (256.0 * _dot(below, jnp.broadcast_to(nt_hi, wide).astype(BF16))
                + _dot(below, jnp.broadcast_to(nt_lo, wide).astype(BF16)))
        base_s[...] = excl[:, 0:1] * tile
        run_s[...] = jnp.zeros_like(run_s)
        cnt_ref[...] = cnt

    @pl.when(phase == 1)
    def _():
        r = lax.broadcasted_iota(I32, (tb, tb), 0)
        c = lax.broadcasted_iota(I32, (tb, tb), 1)
        upto = jnp.where(r <= c, 1.0, 0.0).astype(BF16)
        within = _dot(ohf.astype(BF16), upto)
        rank = within + (run_s[...] + base_s[...] - 1.0)
        pos_ref[...] = jnp.sum(jnp.where(onehot, rank, 0.0), axis=0, keepdims=True).astype(I32)
        run_s[...] += block_cnt


def _plan(cls, tb, tile):
    t = cls.shape[1]
    return pl.pallas_call(
        functools.partial(_plan_kernel, tile=tile),
        grid=(2, t // tb),
        in_specs=[pl.BlockSpec((1, tb), lambda p, i: (0, i))],
        out_specs=[pl.BlockSpec((1, tb), lambda p, i: (0, i * p)),
                   pl.BlockSpec((CLASS_ROWS, 1), lambda p, i: (0, 0))],
        out_shape=[jax.ShapeDtypeStruct((1, t), I32), jax.ShapeDtypeStruct((CLASS_ROWS, 1), F32)],
        scratch_shapes=[pltpu.VMEM((CLASS_ROWS, 1), F32)] * 3,
        compiler_params=_cparams("arbitrary", "arbitrary"),
        name="moe_plan",
    )(cls)


def _invert_kernel(pos_ref, pad_lo_ref, pad_hi_ref, src_ref):
    def clear(i, carry):
        src_ref[i] = 0
        return carry

    def place(t, carry):
        src_ref[pos_ref[t]] = t
        return carry

    for r in range(N_CLASSES + 1):
        lax.fori_loop(pad_lo_ref[r], pad_hi_ref[r], clear, 0)
    lax.fori_loop(0, pos_ref.shape[0], place, 0, unroll=8)


def _invert(pos, pad_lo, pad_hi, n_sorted):
    return pl.pallas_call(
        _invert_kernel,
        grid_spec=pltpu.PrefetchScalarGridSpec(
            num_scalar_prefetch=3, grid=(1,), in_specs=[],
            out_specs=pl.BlockSpec(memory_space=pltpu.SMEM)),
        out_shape=jax.ShapeDtypeStruct((n_sorted,), I32),
        compiler_params=_cparams("arbitrary"),
        name="moe_invert",
    )(pos, pad_lo, pad_hi)


def _start_row_gather(hbm, idx_ref, base, buf, slot, sem, lo, hi, rows=1):
    for j in range(lo, hi):
        src0 = idx_ref[base + j] * rows
        if rows > 1:
            src0 = pl.multiple_of(src0, rows)
        pltpu.make_async_copy(hbm.at[pl.ds(src0, rows)],
                              buf.at[slot, pl.ds(j * rows, rows)], sem).start(priority=j % 2)


def _row_groups(n, parts):
    edges = [n * p // parts for p in range(parts + 1)]
    return list(zip(edges[:-1], edges[1:]))


def _wait_row_gather(hbm, buf, slot, sem, n, rows=1):
    for j in range(n):
        pltpu.make_async_copy(hbm.at[pl.ds(0, rows)],
                              buf.at[slot, pl.ds(j * rows, rows)], sem).wait()


def _experts_kernel(ea_ref, eb_ref, nvalid_ref, src_ref, x_hbm, rwa_ref, rwb_ref,
                    wga_ref, wua_ref, wda_ref, wgb_ref, wub_ref, wdb_ref, ys_ref,
                    xbuf, xb_s, h_s, sems, *, tile):
    i = pl.program_id(0)
    nvalid = nvalid_ref[0]
    last = nvalid - 1
    nchunk = D_MODEL // LANES

    @pl.when(i == 0)
    def _():
        _start_row_gather(x_hbm, src_ref, 0, xbuf, 0, sems.at[0], 0, tile, nchunk)
        _start_row_gather(x_hbm, src_ref, jnp.minimum(1, last) * tile, xbuf, 1, sems.at[1],
                          0, tile, nchunk)

    def run(slot):
        _wait_row_gather(x_hbm, xbuf, slot, sems.at[slot], tile, nchunk)
        base = jnp.minimum(i + 2, last) * tile
        ahead = (slot + 2) % GATHER_SLOTS
        groups = _row_groups(tile, 4)

        def prefetch(p):
            _start_row_gather(x_hbm, src_ref, base, xbuf, ahead, sems.at[ahead], *groups[p], nchunk)

        xcur = xbuf.at[slot]
        xf = jnp.concatenate([xcur[pl.ds(c, tile, stride=nchunk), :] for c in range(nchunk)], axis=1)
        dlogit = jnp.sum(xf * (rwa_ref[0] - rwb_ref[0]), axis=-1, keepdims=True)
        g_lo = 1.0 / (1.0 + jnp.exp(-dlogit))
        g_hi = 1.0 / (1.0 + jnp.exp(dlogit))
        xb_s[...] = xf.astype(BF16)
        prefetch(0)
        x = xb_s[...]
        h_s[:, 0:D_EXPERT] = (jax.nn.silu(_dot(x, wga_ref[0])) * _dot(x, wua_ref[0])
                              * g_lo).astype(BF16)
        prefetch(1)
        h_s[:, D_EXPERT:2 * D_EXPERT] = (jax.nn.silu(_dot(x, wgb_ref[0])) * _dot(x, wub_ref[0])
                                         * g_hi).astype(BF16)
        prefetch(2)
        ys_ref[...] = (_dot(h_s[:, 0:D_EXPERT], wda_ref[0])
                       + _dot(h_s[:, D_EXPERT:2 * D_EXPERT], wdb_ref[0]))
        prefetch(3)

    for s in range(GATHER_SLOTS):
        pl.when((i < nvalid) & (i % GATHER_SLOTS == s))(functools.partial(run, s))

    @pl.when(i == last)
    def _():
        for ahead in (1, 2):
            slot = (i + ahead) % GATHER_SLOTS
            _wait_row_gather(x_hbm, xbuf, slot, sems.at[slot], tile, nchunk)

    @pl.when(i >= nvalid)
    def _():
        ys_ref[...] = jnp.zeros_like(ys_ref)


def _experts(ea, eb, nvalid, src, x, rw_rows, w_gate, w_up, w_down, tile):
    n_sorted = src.shape[0]
    d, f = D_MODEL, D_EXPERT
    rw_a = pl.BlockSpec((1, 1, d), lambda i, ea, eb, nv, src: (ea[i], 0, 0))
    rw_b = pl.BlockSpec((1, 1, d), lambda i, ea, eb, nv, src: (eb[i], 0, 0))
    up_a = pl.BlockSpec((1, d, f), lambda i, ea, eb, nv, src: (ea[i], 0, 0))
    up_b = pl.BlockSpec((1, d, f), lambda i, ea, eb, nv, src: (eb[i], 0, 0))
    dn_a = pl.BlockSpec((1, f, d), lambda i, ea, eb, nv, src: (ea[i], 0, 0))
    dn_b = pl.BlockSpec((1, f, d), lambda i, ea, eb, nv, src: (eb[i], 0, 0))
    return pl.pallas_call(
        functools.partial(_experts_kernel, tile=tile),
        grid_spec=pltpu.PrefetchScalarGridSpec(
            num_scalar_prefetch=4, grid=(n_sorted // tile,),
            in_specs=[pl.BlockSpec(memory_space=pl.ANY), rw_a, rw_b,
                      up_a, up_a, dn_a, up_b, up_b, dn_b],
            out_specs=pl.BlockSpec((tile, d), lambda i, ea, eb, nv, src: (i, 0)),
            scratch_shapes=[pltpu.VMEM((GATHER_SLOTS, tile * (d // LANES), LANES), F32),
                            pltpu.VMEM((tile, d), BF16),
                            pltpu.VMEM((tile, 2 * f), BF16),
                            pltpu.SemaphoreType.DMA((GATHER_SLOTS,))]),
        out_shape=jax.ShapeDtypeStruct((n_sorted, d), F32),
        compiler_params=_cparams("arbitrary"),
        name="moe_experts",
    )(ea, eb, nvalid, src, x, rw_rows, rw_rows, w_gate, w_up, w_down, w_gate, w_up, w_down)


def _ln2_kernel(pos_ref, x_ref, ys_hbm, g_ref, b_ref, *rest, tile, with_qkv):
    if with_qkv:
        w_ref, o_ref, q_ref, k_ref, v_ref, ybuf, sems = rest
    else:
        o_ref, ybuf, sems = rest
    i = pl.program_id(0)
    last = pl.num_programs(0) - 1

    @pl.when(i == 0)
    def _():
        _start_row_gather(ys_hbm, pos_ref, 0, ybuf, 0, sems.at[0], 0, tile)
        _start_row_gather(ys_hbm, pos_ref, jnp.minimum(1, last) * tile, ybuf, 1, sems.at[1], 0, tile)

    def run(slot):
        _wait_row_gather(ys_hbm, ybuf, slot, sems.at[slot], tile)
        base = jnp.minimum(i + 2, last) * tile
        ahead = (slot + 2) % GATHER_SLOTS
        groups = _row_groups(tile, 4 if with_qkv else 1)

        def prefetch(p):
            _start_row_gather(ys_hbm, pos_ref, base, ybuf, ahead, sems.at[ahead], *groups[p])

        z = DEEPNORM_ALPHA * x_ref[...] + ybuf[slot]
        prefetch(0)
        y = _layer_norm(z, g_ref[...], b_ref[...])
        o_ref[...] = y
        if with_qkv:
            d = D_MODEL
            yb = y.astype(BF16)
            q_ref[...] = (_dot(yb, w_ref[:, 0:d]) * (DIFF_DH ** -0.5)).astype(BF16)
            prefetch(1)
            k_ref[...] = _dot(yb, w_ref[:, d:2 * d]).astype(BF16)
            prefetch(2)
            v_ref[...] = _dot(yb, w_ref[:, 2 * d:3 * d]).astype(BF16)
            prefetch(3)

    for s in range(GATHER_SLOTS):
        pl.when(i % GATHER_SLOTS == s)(functools.partial(run, s))

    @pl.when(i == last)
    def _():
        for ahead in (1, 2):
            slot = (i + ahead) % GATHER_SLOTS
            _wait_row_gather(ys_hbm, ybuf, slot, sems.at[slot], tile)


def _ln2(pos, x1, ys, ln_g, ln_b, w_qkv_bf16, tile):
    t, d = x1.shape
    with_qkv = w_qkv_bf16 is not None
    row = pl.BlockSpec((tile, d), lambda i, pos: (i, 0))
    vec = pl.BlockSpec((1, d), lambda i, pos: (0, 0))
    in_specs = [row, pl.BlockSpec(memory_space=pl.ANY), vec, vec]
    out_specs = [row]
    out_shape = [jax.ShapeDtypeStruct((t, d), F32)]
    args = [pos, x1, ys, ln_g, ln_b]
    if with_qkv:
        in_specs.append(pl.BlockSpec((d, 3 * d), lambda i, pos: (0, 0)))
        out_specs += [row, row, row]
        out_shape += [jax.ShapeDtypeStruct((t, d), BF16)] * 3
        args.append(w_qkv_bf16)
    return pl.pallas_call(
        functools.partial(_ln2_kernel, tile=tile, with_qkv=with_qkv),
        grid_spec=pltpu.PrefetchScalarGridSpec(
            num_scalar_prefetch=1, grid=(t // tile,),
            in_specs=in_specs, out_specs=out_specs,
            scratch_shapes=[pltpu.VMEM((GATHER_SLOTS, tile, d), F32), pltpu.SemaphoreType.DMA((GATHER_SLOTS,))]),
        out_shape=out_shape,
        compiler_params=_cparams("arbitrary"),
        name="ln2_qkv" if with_qkv else "ln2",
    )(*args)


def _moe_layer(x1, x1t, cls, rw_rows, w_gate, w_up, w_down, ln_g, ln_b, w_qkv_bf16):
    t = x1.shape[0]
    tile = min(MOE_TILE, t)
    n_tiles = t // tile + N_CLASSES
    pos, cnt = _plan(cls, min(512, t), tile)
    pos = pos.reshape(t)
    cnt = cnt[:N_CLASSES, 0].astype(I32)
    ntile = (cnt + tile - 1) // tile
    tile_end = jnp.cumsum(ntile)
    nvalid = tile_end[-1]
    tid = jnp.minimum(jnp.arange(n_tiles, dtype=I32), nvalid - 1)
    tcls = jnp.sum((tid[:, None] >= tile_end[None, :]).astype(I32), axis=1)
    grp, pair = tcls // PAIRS_PER_GROUP, tcls % PAIRS_PER_GROUP
    ea = grp * EXPERTS_PER_GROUP + jnp.array([0, 0, 0, 1, 1, 2], I32)[pair]
    eb = grp * EXPERTS_PER_GROUP + jnp.array([1, 2, 3, 2, 3, 3], I32)[pair]
    base = (tile_end - ntile) * tile
    n_sorted = n_tiles * tile
    pad_lo = jnp.concatenate([base + cnt, (nvalid * tile).reshape(1)])
    pad_hi = jnp.concatenate([base + ntile * tile, jnp.full((1,), n_sorted, I32)])
    src = _invert(pos, pad_lo, pad_hi, n_sorted)
    ys = _experts(ea, eb, nvalid.reshape(1), src, x1t, rw_rows, w_gate, w_up, w_down, tile)
    return _ln2(pos, x1, ys, ln_g, ln_b, w_qkv_bf16, min(LN2_TILE, t))


def _diff_attn_kernel(q_ref, k_ref, v_ref, lam_ref, sg_ref, o_ref, m_s, acc_s, *, tq, lambda_init):
    qi = pl.program_id(2)
    dh, dv = DIFF_DH, DIFF_DV
    heads = ATTN_HEADS_PER_STEP
    rows = 2 * tq
    lane = lax.broadcasted_iota(I32, (tq, dv), 1)
    qs = []
    for h in range(heads):
        q = q_ref[:, h * dv:(h + 1) * dv]
        zero = jnp.zeros_like(q)
        qs.append(jnp.concatenate([jnp.where(lane < dh, q, zero), jnp.where(lane >= dh, q, zero)],
                                  axis=0))
    m_s[...] = jnp.full_like(m_s, NEG_BIG)
    acc_s[...] = jnp.zeros_like(acc_s)
    ones = jnp.ones((1, dv), BF16)

    def block(r0, width, masked):
        ncol = width // LANES
        for h in range(heads):
            s = _dot_nt(qs[h], k_ref[pl.ds(r0, width), h * dv:(h + 1) * dv])
            if masked:
                row = lax.broadcasted_iota(I32, (tq, width), 0)
                col = lax.broadcasted_iota(I32, (tq, width), 1)
                keep = jnp.concatenate([col <= row, col <= row], axis=0)
                s = jnp.where(keep, s, NEG_BIG)
            cols = [s[:, c * LANES:(c + 1) * LANES] for c in range(ncol)]
            part = cols[0]
            for c in range(1, ncol):
                part = jnp.maximum(part, cols[c])
            m_old = m_s[h]
            m_new = jnp.maximum(m_old, jnp.broadcast_to(jnp.max(part, axis=-1, keepdims=True),
                                                        (rows, LANES)))
            p = jnp.concatenate([jnp.exp((cb - m_new).astype(BF16)) for cb in cols], axis=1)
            v = v_ref[pl.ds(r0, width), h * dv:(h + 1) * dv]
            v_ext = jnp.concatenate([v, jnp.broadcast_to(ones, (width, dv))], axis=1)
            scale = jnp.exp(m_old - m_new)
            acc_s[h] = acc_s[h] * jnp.concatenate([scale, scale], axis=1) + _dot(p, v_ext)
            m_s[h] = m_new

    wide = 2 * tq

    n_wide = qi // 2

    def body(j, carry):
        block(pl.multiple_of(2 * j * wide, wide), wide, False)
        block(pl.multiple_of((2 * j + 1) * wide, wide), wide, False)
        return carry

    lax.fori_loop(0, n_wide // 2, body, 0)

    @pl.when(n_wide % 2 == 1)
    def _():
        block(pl.multiple_of((n_wide - 1) * wide, wide), wide, False)

    @pl.when(qi % 2 == 1)
    def _():
        block(pl.multiple_of((qi - 1) * tq, tq), tq, False)

    block(pl.multiple_of(qi * tq, tq), tq, True)

    lam = lam_ref[...]
    lam_val = (jnp.exp(jnp.sum(lam[0:1] * lam[1:2], axis=1, keepdims=True))
               - jnp.exp(jnp.sum(lam[2:3] * lam[3:4], axis=1, keepdims=True)) + lambda_init)
    for h in range(heads):
        acc = acc_s[h]
        o12 = acc[:, 0:dv] / acc[:, dv:2 * dv]
        o = o12[0:tq] - lam_val * o12[tq:rows]
        o = o * lax.rsqrt(jnp.mean(o * o, axis=-1, keepdims=True) + RMS_EPS) * sg_ref[...]
        o_ref[:, h * dv:(h + 1) * dv] = (o * (1.0 - lambda_init)).astype(BF16)


def _diff_attn(q, k, v, lam, subln_g, batch, seq, tq, lambda_init):
    t, d = q.shape
    nq = seq // tq
    heads = ATTN_HEADS_PER_STEP
    w = heads * DIFF_DV
    return pl.pallas_call(
        functools.partial(_diff_attn_kernel, tq=tq, lambda_init=lambda_init),
        grid=(batch, DIFF_HEADS // heads, nq),
        in_specs=[pl.BlockSpec((tq, w), lambda b, h, i: (b * nq + i, h)),
                  pl.BlockSpec((seq, w), lambda b, h, i: (b, h)),
                  pl.BlockSpec((seq, w), lambda b, h, i: (b, h)),
                  pl.BlockSpec(lam.shape, lambda b, h, i: (0, 0)),
                  pl.BlockSpec((1, DIFF_DV), lambda b, h, i: (0, 0))],
        out_specs=pl.BlockSpec((tq, w), lambda b, h, i: (b * nq + i, h)),
        out_shape=jax.ShapeDtypeStruct((t, d), BF16),
        scratch_shapes=[pltpu.VMEM((heads, 2 * tq, LANES), F32),
                        pltpu.VMEM((heads, 2 * tq, 2 * DIFF_DV), F32)],
        compiler_params=_cparams("parallel", "parallel", "arbitrary"),
        name="diff_attn",
    )(q, k, v, lam, subln_g)


def kernel(x, a_w_in, a_lb, a_norm_g, a_w_out, kv_w, b_w_q, b_lam, b_subln_g, b_w_out,
           ln1_g, ln1_b, ln2_g, ln2_b, router_w, router_b, moe_w_gate, moe_w_up, moe_w_down):
    batch, seq, d = x.shape
    t = batch * seq
    tm = min(512, t)
    xt = x.reshape(t, d)

    rw = jnp.pad(router_w, ((0, 0), (0, LANES - N_EXPERTS)))
    rw_hi = rw.astype(BF16)
    rw_cat = jnp.concatenate([rw_hi, (rw - rw_hi.astype(F32)).astype(BF16)], axis=1)
    rb_col = router_b.reshape(N_EXPERTS, 1)
    rw_rows = router_w.T.reshape(N_EXPERTS, 1, d)
    w_qkv = jnp.concatenate([b_w_q[0], kv_w], axis=1).astype(BF16)

    qkv = None
    for layer in range(DEPTH):
        if layer < N_A_LAYERS:
            q, k, v, g, lf = _hgrn_in(xt, a_w_in[layer].astype(BF16), a_lb, layer, tm)
            o = _hgrn_scan(q, k, v, lf, g, a_norm_g[layer], batch, seq, min(512, seq))
            w_out = a_w_out[layer]
        else:
            j = layer - N_A_LAYERS
            lambda_init = 0.8 - 0.6 * math.exp(-0.3 * layer)
            o = _diff_attn(*qkv, b_lam[j], b_subln_g[j].reshape(1, DIFF_DV), batch, seq,
                           min(ATTN_TQ, seq), lambda_init)
            w_out = b_w_out[j]
        x1, x1t, cls = _post_mix(xt, o, w_out.astype(BF16), ln1_g[layer].reshape(1, d),
                            ln1_b[layer].reshape(1, d), rw_cat, rb_col, tm)
        outs = _moe_layer(x1, x1t, cls, rw_rows, moe_w_gate[layer].astype(BF16),
                          moe_w_up[layer].astype(BF16), moe_w_down[layer].astype(BF16),
                          ln2_g[layer].reshape(1, d), ln2_b[layer].reshape(1, d),
                          w_qkv if layer + 1 == N_A_LAYERS else None)
        xt, qkv = outs[0], outs[1:]
    return xt.reshape(batch, seq, d)
```

```python
import functools
import math

import jax
import jax.numpy as jnp
from jax import lax
from jax.experimental import pallas as pl
from jax.experimental.pallas import tpu as pltpu

F32 = jnp.float32
BF16 = jnp.bfloat16
I32 = jnp.int32

D_MODEL = 1024
DEPTH = 2
N_A_LAYERS = DEPTH // 2
HGRN_HEADS = 8
HGRN_DH = D_MODEL // HGRN_HEADS
HGRN_CHUNK = 64
DIFF_HEADS = 8
DIFF_DH = D_MODEL // (2 * DIFF_HEADS)
DIFF_DV = 2 * DIFF_DH
N_EXPERTS = 16
N_GROUPS = 4
EXPERTS_PER_GROUP = N_EXPERTS // N_GROUPS
PAIRS_PER_GROUP = 6
N_CLASSES = N_GROUPS * PAIRS_PER_GROUP
CLASS_ROWS = 32
D_EXPERT = 512
DEEPNORM_ALPHA = (2.0 * DEPTH) ** 0.25
LN_EPS = 1e-5
RMS_EPS = 1e-6

LANES = 128
MXU_COLS = 256
MOE_TILE = 256
LN2_TILE = 256
GATHER_SLOTS = 3
ATTN_TQ = 256
ATTN_HEADS_PER_STEP = 8
NEG_BIG = -1e30
VMEM_LIMIT = 56 * 1024 * 1024


def _cparams(*sem):
    return pltpu.CompilerParams(dimension_semantics=sem, vmem_limit_bytes=VMEM_LIMIT)


def _dot(a, b):
    return jnp.dot(a, b, preferred_element_type=F32)


def _dot_nt(a, b):
    return lax.dot_general(a, b, (((1,), (1,)), ((), ())), preferred_element_type=F32)


def _dot_tn(a, b):
    return lax.dot_general(a, b, (((0,), (0,)), ((), ())), preferred_element_type=F32)


def _split_bf16(x):
    hi = x.astype(BF16)
    lo = (x - hi.astype(F32)).astype(BF16)
    return hi, lo


def _layer_norm(z, g, b):
    mu = jnp.mean(z, axis=-1, keepdims=True)
    zc = z - mu
    var = jnp.mean(zc * zc, axis=-1, keepdims=True)
    return zc * lax.rsqrt(var + LN_EPS) * g + b


def _hgrn_in_kernel(x_ref, w_ref, lb_ref, q_ref, k_ref, v_ref, g_ref, lf_ref, *, layer):
    d = D_MODEL
    x = x_ref[...].astype(BF16)
    a = lb_ref[...]
    e = jnp.exp(a - jnp.max(a, axis=0, keepdims=True))
    sm = e / jnp.sum(e, axis=0, keepdims=True)
    lb = jnp.sum(sm[0:layer + 1], axis=0, keepdims=True)
    q_ref[...] = _dot(x, w_ref[:, 0:d]).astype(BF16)
    f = lb + (1.0 - lb) * jax.nn.sigmoid(_dot(x, w_ref[:, d:2 * d]))
    lf_ref[...] = jnp.log(f)
    k_ref[...] = (1.0 - f).astype(BF16)
    v_ref[...] = _dot(x, w_ref[:, 2 * d:3 * d]).astype(BF16)
    g_ref[...] = jax.nn.sigmoid(_dot(x, w_ref[:, 3 * d:4 * d])).astype(BF16)


def _hgrn_in(x, w_in_bf16, a_lb, layer, tm):
    t, d = x.shape
    row = pl.BlockSpec((tm, d), lambda i: (i, 0))
    return pl.pallas_call(
        functools.partial(_hgrn_in_kernel, layer=layer),
        grid=(t // tm,),
        in_specs=[row,
                  pl.BlockSpec((d, 4 * d), lambda i: (0, 0)),
                  pl.BlockSpec(a_lb.shape, lambda i: (0, 0))],
        out_specs=[row, row, row, row, row],
        out_shape=[jax.ShapeDtypeStruct((t, d), BF16)] * 4 + [jax.ShapeDtypeStruct((t, d), F32)],
        compiler_params=_cparams("parallel"),
        name="hgrn_in",
    )(x, w_in_bf16, a_lb)


def _hgrn_scan_kernel(q_ref, k_ref, v_ref, lf_ref, g_ref, ng_ref, o_ref, state_ref, *, n_chunks):
    c_len, dh = HGRN_CHUNK, HGRN_DH

    @pl.when(pl.program_id(1) == 0)
    def _():
        state_ref[...] = jnp.zeros_like(state_ref)

    row = lax.broadcasted_iota(I32, (c_len, c_len), 0)
    col = lax.broadcasted_iota(I32, (c_len, c_len), 1)
    causal = col <= row
    tri = jnp.where(causal, 1.0, 0.0).astype(BF16)

    heads = range(HGRN_HEADS)
    cols = [slice(h * dh, (h + 1) * dh) for h in heads]

    def chunk(c, carry):
        r0 = pl.multiple_of(c * c_len, c_len)
        rows = pl.ds(r0, c_len)
        lf_hi, lf_lo = _split_bf16(lf_ref[rows, :])
        b_all = _dot(tri, lf_hi) + _dot(tri, lf_lo)
        b = [b_all[:, cs] for cs in cols]
        q = [q_ref[rows, cs].astype(F32) for cs in cols]
        k = [k_ref[rows, cs].astype(F32) for cs in cols]
        v = [v_ref[rows, cs] for cs in cols]
        b_mid = [x[c_len // 2 - 1:c_len // 2, :] for x in b]
        b_last = [x[c_len - 1:c_len, :] for x in b]
        e_up = [jnp.exp(b[h] - b_mid[h]) for h in heads]
        e_dn = [jnp.exp(b_mid[h] - b[h]) for h in heads]
        qs = [q[h] * e_up[h] for h in heads]
        ks = [k[h] * e_dn[h] for h in heads]
        s = [jnp.where(causal, _dot_nt(qs[h].astype(BF16), ks[h].astype(BF16)), 0.0).astype(BF16)
             for h in heads]
        st = [state_ref[h] for h in heads]
        qd = [(qs[h] * jnp.exp(b_mid[h])).astype(BF16) for h in heads]
        o = [_dot(s[h], v[h]) + _dot_nt(qd[h], st[h].astype(BF16)) for h in heads]
        kd = [(ks[h] * jnp.exp(b_last[h] - b_mid[h])).astype(BF16) for h in heads]
        for h in heads:
            state_ref[h] = jnp.exp(b_last[h]) * st[h] + _dot_tn(v[h], kd[h])
        for h in heads:
            on = o[h] * lax.rsqrt(jnp.mean(o[h] * o[h], axis=-1, keepdims=True) + RMS_EPS)
            o_ref[rows, cols[h]] = (on * ng_ref[h:h + 1, :]
                                    * g_ref[rows, cols[h]].astype(F32)).astype(BF16)
        return carry

    lax.fori_loop(0, n_chunks, chunk, 0, unroll=2)


def _hgrn_scan(q, k, v, lf, g, norm_g, batch, seq, rows_per_step):
    t, d = q.shape
    steps = seq // rows_per_step
    blk = pl.BlockSpec((rows_per_step, d), lambda b, s: (b * steps + s, 0))
    return pl.pallas_call(
        functools.partial(_hgrn_scan_kernel, n_chunks=rows_per_step // HGRN_CHUNK),
        grid=(batch, steps),
        in_specs=[blk, blk, blk, blk, blk, pl.BlockSpec(norm_g.shape, lambda b, s: (0, 0))],
        out_specs=blk,
        out_shape=jax.ShapeDtypeStruct((t, d), BF16),
        scratch_shapes=[pltpu.VMEM((HGRN_HEADS, HGRN_DH, HGRN_DH), F32)],
        compiler_params=_cparams("parallel", "arbitrary"),
        name="hgrn_scan",
    )(q, k, v, lf, g, norm_g)


def _route_class(logits_t, rb):
    e = jnp.exp(logits_t - jnp.max(logits_t, axis=0, keepdims=True))
    sel = e / jnp.sum(e, axis=0, keepdims=True) + rb
    selr = [sel[i:i + 1, :] for i in range(N_EXPERTS)]
    n_per = EXPERTS_PER_GROUP

    def top2_sum(vals):
        best = None
        for i in range(len(vals)):
            for j in range(i + 1, len(vals)):
                pair = vals[i] + vals[j]
                best = pair if best is None else jnp.maximum(best, pair)
        return best

    gscore = [top2_sum(selr[g * n_per:(g + 1) * n_per]) for g in range(N_GROUPS)]
    gbest = jnp.zeros_like(gscore[0], dtype=I32)
    gval = gscore[0]
    for g in range(1, N_GROUPS):
        better = gscore[g] > gval
        gbest = jnp.where(better, g, gbest)
        gval = jnp.where(better, gscore[g], gval)

    def in_best_group(i):
        out = selr[(N_GROUPS - 1) * n_per + i]
        for g in range(N_GROUPS - 2, -1, -1):
            out = jnp.where(gbest == g, selr[g * n_per + i], out)
        return out

    sv = [in_best_group(i) for i in range(n_per)]
    i1 = jnp.zeros_like(gbest)
    v1 = sv[0]
    for i in range(1, n_per):
        better = sv[i] > v1
        i1 = jnp.where(better, i, i1)
        v1 = jnp.where(better, sv[i], v1)
    i2 = jnp.full_like(gbest, -1)
    v2 = jnp.full_like(v1, -jnp.inf)
    for i in range(n_per):
        better = (i1 != i) & (sv[i] > v2)
        i2 = jnp.where(better, i, i2)
        v2 = jnp.where(better, sv[i], v2)
    lo = jnp.minimum(i1, i2)
    hi = jnp.maximum(i1, i2)
    pair = jnp.where(lo == 0, hi - 1, jnp.where(lo == 1, hi + 1, PAIRS_PER_GROUP - 1))
    return gbest * PAIRS_PER_GROUP + pair


def _post_mix_kernel(x_ref, o_ref, w_ref, g_ref, b_ref, rw_ref, rb_ref, y_ref, yt_ref, cls_ref,
                     *, sub):
    nchunk = D_MODEL // LANES
    for r0 in range(0, x_ref.shape[0], sub):
        rows = slice(r0, r0 + sub)
        z = DEEPNORM_ALPHA * x_ref[rows, :] + _dot(o_ref[rows, :], w_ref[...])
        y = _layer_norm(z, g_ref[...], b_ref[...])
        y_ref[rows, :] = y
        for c in range(nchunk):
            yt_ref[pl.ds(r0 * nchunk + c, sub, stride=nchunk), :] = y[:, c * LANES:(c + 1) * LANES]
        yh, yl = _split_bf16(y)
        both = _dot(yh, rw_ref[...])
        logits = both[:, 0:LANES] + both[:, LANES:2 * LANES] + _dot(yl, rw_ref[:, 0:LANES])
        cls_ref[:, rows] = _route_class(jnp.transpose(logits)[0:N_EXPERTS, :], rb_ref[...])


def _post_mix(x, o, w_out_bf16, ln_g, ln_b, rw_cat, rb_col, tm):
    t, d = o.shape
    row = pl.BlockSpec((tm, d), lambda i: (i, 0))
    vec = pl.BlockSpec((1, d), lambda i: (0, 0))
    return pl.pallas_call(
        functools.partial(_post_mix_kernel, sub=min(256, tm)),
        grid=(t // tm,),
        in_specs=[row, row,
                  pl.BlockSpec((d, d), lambda i: (0, 0)),
                  vec, vec,
                  pl.BlockSpec((d, 2 * LANES), lambda i: (0, 0)),
                  pl.BlockSpec((N_EXPERTS, 1), lambda i: (0, 0))],
        out_specs=[row, pl.BlockSpec((tm * (d // LANES), LANES), lambda i: (i, 0)),
                   pl.BlockSpec((1, tm), lambda i: (0, i))],
        out_shape=[jax.ShapeDtypeStruct((t, d), F32),
                   jax.ShapeDtypeStruct((t * (d // LANES), LANES), F32),
                   jax.ShapeDtypeStruct((1, t), I32)],
        compiler_params=_cparams("parallel"),
        name="post_mix",
    )(x, o, w_out_bf16, ln_g, ln_b, rw_cat, rb_col)


def _plan_kernel(cls_ref, pos_ref, cnt_ref, cnt_s, base_s, run_s, *, tile):
    phase, i = pl.program_id(0), pl.program_id(1)
    tb = cls_ref.shape[1]
    onehot = cls_ref[...] == lax.broadcasted_iota(I32, (CLASS_ROWS, tb), 0)
    ohf = jnp.where(onehot, 1.0, 0.0)
    block_cnt = jnp.sum(ohf, axis=1, keepdims=True)

    @pl.when((phase == 0) & (i == 0))
    def _():
        cnt_s[...] = jnp.zeros_like(cnt_s)

    @pl.when(phase == 0)
    def _():
        cnt_s[...] += block_cnt

    @pl.when((phase == 1) & (i == 0))
    def _():
        cnt = cnt_s[...]
        ntile = jnp.floor((cnt + (tile - 1)) * (1.0 / tile))
        nt_hi = jnp.floor(ntile * (1.0 / 256.0))
        nt_lo = ntile - 256.0 * nt_hi
        r = lax.broadcasted_iota(I32, (CLASS_ROWS, CLASS_ROWS), 0)
        c = lax.broadcasted_iota(I32, (CLASS_ROWS, CLASS_ROWS), 1)
        below = jnp.where(c < r, 1.0, 0.0).astype(BF16)
        wide = (CLASS_ROWS, LANES)
        excl = (256.0 * _dot(below, jnp.broadcast_to(nt_hi, wide).astype(BF16))
                + _dot(below, jnp.broadcast_to(nt_lo, wide).astype(BF16)))
        base_s[...] = excl[:, 0:1] * tile
        run_s[...] = jnp.zeros_like(run_s)
        cnt_ref[...] = cnt

    @pl.when(phase == 1)
    def _():
        r = lax.broadcasted_iota(I32, (tb, tb), 0)
        c = lax.broadcasted_iota(I32, (tb, tb), 1)
        upto = jnp.where(r <= c, 1.0, 0.0).astype(BF16)
        within = _dot(ohf.astype(BF16), upto)
        rank = within + (run_s[...] + base_s[...] - 1.0)
        pos_ref[...] = jnp.sum(jnp.where(onehot, rank, 0.0), axis=0, keepdims=True).astype(I32)
        run_s[...] += block_cnt


def _plan(cls, tb, tile):
    t = cls.shape[1]
    return pl.pallas_call(
        functools.partial(_plan_kernel, tile=tile),
        grid=(2, t // tb),
        in_specs=[pl.BlockSpec((1, tb), lambda p, i: (0, i))],
        out_specs=[pl.BlockSpec((1, tb), lambda p, i: (0, i * p)),
                   pl.BlockSpec((CLASS_ROWS, 1), lambda p, i: (0, 0))],
        out_shape=[jax.ShapeDtypeStruct((1, t), I32), jax.ShapeDtypeStruct((CLASS_ROWS, 1), F32)],
        scratch_shapes=[pltpu.VMEM((CLASS_ROWS, 1), F32)] * 3,
        compiler_params=_cparams("arbitrary", "arbitrary"),
        name="moe_plan",
    )(cls)


def _invert_kernel(pos_ref, pad_lo_ref, pad_hi_ref, src_ref):
    def clear(i, carry):
        src_ref[i] = 0
        return carry

    def place(t, carry):
        src_ref[pos_ref[t]] = t
        return carry

    for r in range(N_CLASSES + 1):
        lax.fori_loop(pad_lo_ref[r], pad_hi_ref[r], clear, 0)
    lax.fori_loop(0, pos_ref.shape[0], place, 0, unroll=8)


def _invert(pos, pad_lo, pad_hi, n_sorted):
    return pl.pallas_call(
        _invert_kernel,
        grid_spec=pltpu.PrefetchScalarGridSpec(
            num_scalar_prefetch=3, grid=(1,), in_specs=[],
            out_specs=pl.BlockSpec(memory_space=pltpu.SMEM)),
        out_shape=jax.ShapeDtypeStruct((n_sorted,), I32),
        compiler_params=_cparams("arbitrary"),
        name="moe_invert",
    )(pos, pad_lo, pad_hi)


def _start_row_gather(hbm, idx_ref, base, buf, slot, sem, lo, hi, rows=1):
    for j in range(lo, hi):
        src0 = idx_ref[base + j] * rows
        if rows > 1:
            src0 = pl.multiple_of(src0, rows)
        pltpu.make_async_copy(hbm.at[pl.ds(src0, rows)],
                              buf.at[slot, pl.ds(j * rows, rows)], sem).start(priority=j % 2)


def _row_groups(n, parts):
    edges = [n * p // parts for p in range(parts + 1)]
    return list(zip(edges[:-1], edges[1:]))


def _wait_row_gather(hbm, buf, slot, sem, n, rows=1):
    for j in range(n):
        pltpu.make_async_copy(hbm.at[pl.ds(0, rows)],
                              buf.at[slot, pl.ds(j * rows, rows)], sem).wait()


def _experts_kernel(ea_ref, eb_ref, nvalid_ref, src_ref, x_hbm, rwa_ref, rwb_ref,
                    wga_ref, wua_ref, wda_ref, wgb_ref, wub_ref, wdb_ref, ys_ref,
                    xbuf, xb_s, h_s, sems, *, tile):
    i = pl.program_id(0)
    nvalid = nvalid_ref[0]
    last = nvalid - 1
    nchunk = D_MODEL // LANES

    @pl.when(i == 0)
    def _():
        _start_row_gather(x_hbm, src_ref, 0, xbuf, 0, sems.at[0], 0, tile, nchunk)
        _start_row_gather(x_hbm, src_ref, jnp.minimum(1, last) * tile, xbuf, 1, sems.at[1],
                          0, tile, nchunk)

    def run(slot):
        _wait_row_gather(x_hbm, xbuf, slot, sems.at[slot], tile, nchunk)
        base = jnp.minimum(i + 2, last) * tile
        ahead = (slot + 2) % GATHER_SLOTS
        f, wn = D_EXPERT, MXU_COLS
        n_up, n_dn = f // wn, D_MODEL // wn
        groups = iter(_row_groups(tile, 1 + 2 * n_up + n_dn))

        def prefetch():
            _start_row_gather(x_hbm, src_ref, base, xbuf, ahead, sems.at[ahead], *next(groups), nchunk)

        xcur = xbuf.at[slot]
        xf = jnp.concatenate([xcur[pl.ds(c, tile, stride=nchunk), :] for c in range(nchunk)], axis=1)
        dlogit = jnp.sum(xf * (rwa_ref[0] - rwb_ref[0]), axis=-1, keepdims=True)
        g_lo = 1.0 / (1.0 + jnp.exp(-dlogit))
        g_hi = 1.0 / (1.0 + jnp.exp(dlogit))
        xb_s[...] = xf.astype(BF16)
        prefetch()
        x = xb_s[...]
        for e, (wg_ref, wu_ref, gate) in enumerate(((wga_ref, wua_ref, g_lo), (wgb_ref, wub_ref, g_hi))):
            for n in range(n_up):
                cols = slice(n * wn, (n + 1) * wn)
                h = jax.nn.silu(_dot(x, wg_ref[0, :, cols])) * _dot(x, wu_ref[0, :, cols]) * gate
                h_s[:, e * f + n * wn:e * f + (n + 1) * wn] = h.astype(BF16)
                prefetch()
        for n in range(n_dn):
            cols = slice(n * wn, (n + 1) * wn)
            ys_ref[:, cols] = (_dot(h_s[:, 0:f], wda_ref[0, :, cols])
                               + _dot(h_s[:, f:2 * f], wdb_ref[0, :, cols]))
            prefetch()

    for s in range(GATHER_SLOTS):
        pl.when((i < nvalid) & (i % GATHER_SLOTS == s))(functools.partial(run, s))

    @pl.when(i == last)
    def _():
        for ahead in (1, 2):
            slot = (i + ahead) % GATHER_SLOTS
            _wait_row_gather(x_hbm, xbuf, slot, sems.at[slot], tile, nchunk)

    @pl.when(i >= nvalid)
    def _():
        ys_ref[...] = jnp.zeros_like(ys_ref)


def _experts(ea, eb, nvalid, src, x, rw_rows, w_gate, w_up, w_down, tile):
    n_sorted = src.shape[0]
    d, f = D_MODEL, D_EXPERT
    rw_a = pl.BlockSpec((1, 1, d), lambda i, ea, eb, nv, src: (ea[i], 0, 0))
    rw_b = pl.BlockSpec((1, 1, d), lambda i, ea, eb, nv, src: (eb[i], 0, 0))
    up_a = pl.BlockSpec((1, d, f), lambda i, ea, eb, nv, src: (ea[i], 0, 0))
    up_b = pl.BlockSpec((1, d, f), lambda i, ea, eb, nv, src: (eb[i], 0, 0))
    dn_a = pl.BlockSpec((1, f, d), lambda i, ea, eb, nv, src: (ea[i], 0, 0))
    dn_b = pl.BlockSpec((1, f, d), lambda i, ea, eb, nv, src: (eb[i], 0, 0))
    return pl.pallas_call(
        functools.partial(_experts_kernel, tile=tile),
        grid_spec=pltpu.PrefetchScalarGridSpec(
            num_scalar_prefetch=4, grid=(n_sorted // tile,),
            in_specs=[pl.BlockSpec(memory_space=pl.ANY), rw_a, rw_b,
                      up_a, up_a, dn_a, up_b, up_b, dn_b],
            out_specs=pl.BlockSpec((tile, d), lambda i, ea, eb, nv, src: (i, 0)),
            scratch_shapes=[pltpu.VMEM((GATHER_SLOTS, tile * (d // LANES), LANES), F32),
                            pltpu.VMEM((tile, d), BF16),
                            pltpu.VMEM((tile, 2 * f), BF16),
                            pltpu.SemaphoreType.DMA((GATHER_SLOTS,))]),
        out_shape=jax.ShapeDtypeStruct((n_sorted, d), F32),
        compiler_params=_cparams("arbitrary"),
        name="moe_experts",
    )(ea, eb, nvalid, src, x, rw_rows, rw_rows, w_gate, w_up, w_down, w_gate, w_up, w_down)


def _ln2_kernel(pos_ref, x_ref, ys_hbm, g_ref, b_ref, *rest, tile, with_qkv):
    if with_qkv:
        w_ref, o_ref, q_ref, k_ref, v_ref, ybuf, sems = rest
    else:
        o_ref, ybuf, sems = rest
    i = pl.program_id(0)
    last = pl.num_programs(0) - 1

    @pl.when(i == 0)
    def _():
        _start_row_gather(ys_hbm, pos_ref, 0, ybuf, 0, sems.at[0], 0, tile)
        _start_row_gather(ys_hbm, pos_ref, jnp.minimum(1, last) * tile, ybuf, 1, sems.at[1], 0, tile)

    def run(slot):
        _wait_row_gather(ys_hbm, ybuf, slot, sems.at[slot], tile)
        base = jnp.minimum(i + 2, last) * tile
        ahead = (slot + 2) % GATHER_SLOTS
        d, wn = D_MODEL, MXU_COLS
        groups = iter(_row_groups(tile, 1 + 3 * (d // wn) if with_qkv else 1))

        def prefetch():
            _start_row_gather(ys_hbm, pos_ref, base, ybuf, ahead, sems.at[ahead], *next(groups))

        z = DEEPNORM_ALPHA * x_ref[...] + ybuf[slot]
        prefetch()
        y = _layer_norm(z, g_ref[...], b_ref[...])
        o_ref[...] = y
        if with_qkv:
            yb = y.astype(BF16)
            for p, (out_ref, scale) in enumerate(((q_ref, DIFF_DH ** -0.5), (k_ref, None), (v_ref, None))):
                for n in range(d // wn):
                    cols = slice(n * wn, (n + 1) * wn)
                    r = _dot(yb, w_ref[:, p * d + n * wn:p * d + (n + 1) * wn])
                    out_ref[:, cols] = (r if scale is None else r * scale).astype(BF16)
                    prefetch()

    for s in range(GATHER_SLOTS):
        pl.when(i % GATHER_SLOTS == s)(functools.partial(run, s))

    @pl.when(i == last)
    def _():
        for ahead in (1, 2):
            slot = (i + ahead) % GATHER_SLOTS
            _wait_row_gather(ys_hbm, ybuf, slot, sems.at[slot], tile)


def _ln2(pos, x1, ys, ln_g, ln_b, w_qkv_bf16, tile):
    t, d = x1.shape
    with_qkv = w_qkv_bf16 is not None
    row = pl.BlockSpec((tile, d), lambda i, pos: (i, 0))
    vec = pl.BlockSpec((1, d), lambda i, pos: (0, 0))
    in_specs = [row, pl.BlockSpec(memory_space=pl.ANY), vec, vec]
    out_specs = [row]
    out_shape = [jax.ShapeDtypeStruct((t, d), F32)]
    args = [pos, x1, ys, ln_g, ln_b]
    if with_qkv:
        in_specs.append(pl.BlockSpec((d, 3 * d), lambda i, pos: (0, 0)))
        out_specs += [row, row, row]
        out_shape += [jax.ShapeDtypeStruct((t, d), BF16)] * 3
        args.append(w_qkv_bf16)
    return pl.pallas_call(
        functools.partial(_ln2_kernel, tile=tile, with_qkv=with_qkv),
        grid_spec=pltpu.PrefetchScalarGridSpec(
            num_scalar_prefetch=1, grid=(t // tile,),
            in_specs=in_specs, out_specs=out_specs,
            scratch_shapes=[pltpu.VMEM((GATHER_SLOTS, tile, d), F32), pltpu.SemaphoreType.DMA((GATHER_SLOTS,))]),
        out_shape=out_shape,
        compiler_params=_cparams("arbitrary"),
        name="ln2_qkv" if with_qkv else "ln2",
    )(*args)


def _moe_layer(x1, x1t, cls, rw_rows, w_gate, w_up, w_down, ln_g, ln_b, w_qkv_bf16):
    t = x1.shape[0]
    tile = min(MOE_TILE, t)
    n_tiles = t // tile + N_CLASSES
    pos, cnt = _plan(cls, min(512, t), tile)
    pos = pos.reshape(t)
    cnt = cnt[:N_CLASSES, 0].astype(I32)
    ntile = (cnt + tile - 1) // tile
    tile_end = jnp.cumsum(ntile)
    nvalid = tile_end[-1]
    tid = jnp.minimum(jnp.arange(n_tiles, dtype=I32), nvalid - 1)
    tcls = jnp.sum((tid[:, None] >= tile_end[None, :]).astype(I32), axis=1)
    grp, pair = tcls // PAIRS_PER_GROUP, tcls % PAIRS_PER_GROUP
    ea = grp * EXPERTS_PER_GROUP + jnp.array([0, 0, 0, 1, 1, 2], I32)[pair]
    eb = grp * EXPERTS_PER_GROUP + jnp.array([1, 2, 3, 2, 3, 3], I32)[pair]
    base = (tile_end - ntile) * tile
    n_sorted = n_tiles * tile
    pad_lo = jnp.concatenate([base + cnt, (nvalid * tile).reshape(1)])
    pad_hi = jnp.concatenate([base + ntile * tile, jnp.full((1,), n_sorted, I32)])
    src = _invert(pos, pad_lo, pad_hi, n_sorted)
    ys = _experts(ea, eb, nvalid.reshape(1), src, x1t, rw_rows, w_gate, w_up, w_down, tile)
    return _ln2(pos, x1, ys, ln_g, ln_b, w_qkv_bf16, min(LN2_TILE, t))


def _diff_attn_kernel(q_ref, k_ref, v_ref, lam_ref, sg_ref, o_ref, m_s, acc_s, *, tq, lambda_init):
    qi = pl.program_id(2)
    dh, dv = DIFF_DH, DIFF_DV
    heads = ATTN_HEADS_PER_STEP
    rows = 2 * tq
    lane = lax.broadcasted_iota(I32, (tq, dv), 1)
    qs = []
    for h in range(heads):
        q = q_ref[:, h * dv:(h + 1) * dv]
        zero = jnp.zeros_like(q)
        qs.append(jnp.concatenate([jnp.where(lane < dh, q, zero), jnp.where(lane >= dh, q, zero)],
                                  axis=0))
    m_s[...] = jnp.full_like(m_s, NEG_BIG)
    acc_s[...] = jnp.zeros_like(acc_s)
    ones = jnp.ones((1, dv), BF16)

    def block(r0, width, masked):
        ncol = width // LANES
        for h in range(heads):
            s = _dot_nt(qs[h], k_ref[pl.ds(r0, width), h * dv:(h + 1) * dv])
            if masked:
                row = lax.broadcasted_iota(I32, (tq, width), 0)
                col = lax.broadcasted_iota(I32, (tq, width), 1)
                keep = jnp.concatenate([col <= row, col <= row], axis=0)
                s = jnp.where(keep, s, NEG_BIG)
            cols = [s[:, c * LANES:(c + 1) * LANES] for c in range(ncol)]
            part = cols[0]
            for c in range(1, ncol):
                part = jnp.maximum(part, cols[c])
            m_old = m_s[h]
            m_new = jnp.maximum(m_old, jnp.broadcast_to(jnp.max(part, axis=-1, keepdims=True),
                                                        (rows, LANES)))
            p = jnp.concatenate([jnp.exp((cb - m_new).astype(BF16)) for cb in cols], axis=1)
            v = v_ref[pl.ds(r0, width), h * dv:(h + 1) * dv]
            v_ext = jnp.concatenate([v, jnp.broadcast_to(ones, (width, dv))], axis=1)
            scale = jnp.exp(m_old - m_new)
            acc_s[h] = acc_s[h] * jnp.concatenate([scale, scale], axis=1) + _dot(p, v_ext)
            m_s[h] = m_new

    wide = 2 * tq

    n_wide = qi // 2

    def body(j, carry):
        block(pl.multiple_of(2 * j * wide, wide), wide, False)
        block(pl.multiple_of((2 * j + 1) * wide, wide), wide, False)
        return carry

    lax.fori_loop(0, n_wide // 2, body, 0)

    @pl.when(n_wide % 2 == 1)
    def _():
        block(pl.multiple_of((n_wide - 1) * wide, wide), wide, False)

    @pl.when(qi % 2 == 1)
    def _():
        block(pl.multiple_of((qi - 1) * tq, tq), tq, False)

    block(pl.multiple_of(qi * tq, tq), tq, True)

    lam = lam_ref[...]
    lam_val = (jnp.exp(jnp.sum(lam[0:1] * lam[1:2], axis=1, keepdims=True))
               - jnp.exp(jnp.sum(lam[2:3] * lam[3:4], axis=1, keepdims=True)) + lambda_init)
    for h in range(heads):
        acc = acc_s[h]
        o12 = acc[:, 0:dv] / acc[:, dv:2 * dv]
        o = o12[0:tq] - lam_val * o12[tq:rows]
        o = o * lax.rsqrt(jnp.mean(o * o, axis=-1, keepdims=True) + RMS_EPS) * sg_ref[...]
        o_ref[:, h * dv:(h + 1) * dv] = (o * (1.0 - lambda_init)).astype(BF16)


def _diff_attn(q, k, v, lam, subln_g, batch, seq, tq, lambda_init):
    t, d = q.shape
    nq = seq // tq
    heads = ATTN_HEADS_PER_STEP
    w = heads * DIFF_DV
    return pl.pallas_call(
        functools.partial(_diff_attn_kernel, tq=tq, lambda_init=lambda_init),
        grid=(batch, DIFF_HEADS // heads, nq),
        in_specs=[pl.BlockSpec((tq, w), lambda b, h, i: (b * nq + i, h)),
                  pl.BlockSpec((seq, w), lambda b, h, i: (b, h)),
                  pl.BlockSpec((seq, w), lambda b, h, i: (b, h)),
                  pl.BlockSpec(lam.shape, lambda b, h, i: (0, 0)),
                  pl.BlockSpec((1, DIFF_DV), lambda b, h, i: (0, 0))],
        out_specs=pl.BlockSpec((tq, w), lambda b, h, i: (b * nq + i, h)),
        out_shape=jax.ShapeDtypeStruct((t, d), BF16),
        scratch_shapes=[pltpu.VMEM((heads, 2 * tq, LANES), F32),
                        pltpu.VMEM((heads, 2 * tq, 2 * DIFF_DV), F32)],
        compiler_params=_cparams("parallel", "parallel", "arbitrary"),
        name="diff_attn",
    )(q, k, v, lam, subln_g)


def kernel(x, a_w_in, a_lb, a_norm_g, a_w_out, kv_w, b_w_q, b_lam, b_subln_g, b_w_out,
           ln1_g, ln1_b, ln2_g, ln2_b, router_w, router_b, moe_w_gate, moe_w_up, moe_w_down):
    batch, seq, d = x.shape
    t = batch * seq
    tm = min(512, t)
    xt = x.reshape(t, d)

    rw = jnp.pad(router_w, ((0, 0), (0, LANES - N_EXPERTS)))
    rw_hi = rw.astype(BF16)
    rw_cat = jnp.concatenate([rw_hi, (rw - rw_hi.astype(F32)).astype(BF16)], axis=1)
    rb_col = router_b.reshape(N_EXPERTS, 1)
    rw_rows = router_w.T.reshape(N_EXPERTS, 1, d)
    w_qkv = jnp.concatenate([b_w_q[0], kv_w], axis=1).astype(BF16)

    qkv = None
    for layer in range(DEPTH):
        if layer < N_A_LAYERS:
            q, k, v, g, lf = _hgrn_in(xt, a_w_in[layer].astype(BF16), a_lb, layer, tm)
            o = _hgrn_scan(q, k, v, lf, g, a_norm_g[layer], batch, seq, min(512, seq))
            w_out = a_w_out[layer]
        else:
            j = layer - N_A_LAYERS
            lambda_init = 0.8 - 0.6 * math.exp(-0.3 * layer)
            o = _diff_attn(*qkv, b_lam[j], b_subln_g[j].reshape(1, DIFF_DV), batch, seq,
                           min(ATTN_TQ, seq), lambda_init)
            w_out = b_w_out[j]
        x1, x1t, cls = _post_mix(xt, o, w_out.astype(BF16), ln1_g[layer].reshape(1, d),
                            ln1_b[layer].reshape(1, d), rw_cat, rb_col, tm)
        outs = _moe_layer(x1, x1t, cls, rw_rows, moe_w_gate[layer].astype(BF16),
                          moe_w_up[layer].astype(BF16), moe_w_down[layer].astype(BF16),
                          ln2_g[layer].reshape(1, d), ln2_b[layer].reshape(1, d),
                          w_qkv if layer + 1 == N_A_LAYERS else None)
        xt, qkv = outs[0], outs[1:]
    return xt.reshape(batch, seq, d)
```

```python
import functools
import math

import jax
import jax.numpy as jnp
from jax import lax
from jax.experimental import pallas as pl
from jax.experimental.pallas import tpu as pltpu

F32 = jnp.float32
BF16 = jnp.bfloat16
I32 = jnp.int32

D_MODEL = 1024
DEPTH = 2
N_A_LAYERS = DEPTH // 2
HGRN_HEADS = 8
HGRN_DH = D_MODEL // HGRN_HEADS
HGRN_CHUNK = 64
DIFF_HEADS = 8
DIFF_DH = D_MODEL // (2 * DIFF_HEADS)
DIFF_DV = 2 * DIFF_DH
N_EXPERTS = 16
N_GROUPS = 4
EXPERTS_PER_GROUP = N_EXPERTS // N_GROUPS
PAIRS_PER_GROUP = 6
N_CLASSES = N_GROUPS * PAIRS_PER_GROUP
CLASS_ROWS = 32
D_EXPERT = 512
DEEPNORM_ALPHA = (2.0 * DEPTH) ** 0.25
LN_EPS = 1e-5
RMS_EPS = 1e-6

LANES = 128
MXU_COLS = 256
MOE_TILE = 256
LN2_TILE = 256
GATHER_SLOTS = 3
ATTN_TQ = 256
ATTN_HEADS_PER_STEP = 8
NEG_BIG = -1e30
VMEM_LIMIT = 56 * 1024 * 1024


def _cparams(*sem):
    return pltpu.CompilerParams(dimension_semantics=sem, vmem_limit_bytes=VMEM_LIMIT)


def _dot(a, b):
    return jnp.dot(a, b, preferred_element_type=F32)


def _dot_nt(a, b):
    return lax.dot_general(a, b, (((1,), (1,)), ((), ())), preferred_element_type=F32)


def _dot_tn(a, b):
    return lax.dot_general(a, b, (((0,), (0,)), ((), ())), preferred_element_type=F32)


def _split_bf16(x):
    hi = x.astype(BF16)
    lo = (x - hi.astype(F32)).astype(BF16)
    return hi, lo


def _layer_norm(z, g, b):
    mu = jnp.mean(z, axis=-1, keepdims=True)
    zc = z - mu
    var = jnp.mean(zc * zc, axis=-1, keepdims=True)
    return zc * lax.rsqrt(var + LN_EPS) * g + b


def _hgrn_in_kernel(x_ref, w_ref, lb_ref, q_ref, k_ref, v_ref, g_ref, lf_ref, *, layer):
    d = D_MODEL
    x = x_ref[...].astype(BF16)
    a = lb_ref[...]
    e = jnp.exp(a - jnp.max(a, axis=0, keepdims=True))
    sm = e / jnp.sum(e, axis=0, keepdims=True)
    lb = jnp.sum(sm[0:layer + 1], axis=0, keepdims=True)
    q_ref[...] = _dot(x, w_ref[:, 0:d]).astype(BF16)
    f = lb + (1.0 - lb) * jax.nn.sigmoid(_dot(x, w_ref[:, d:2 * d]))
    lf_ref[...] = jnp.log(f)
    k_ref[...] = (1.0 - f).astype(BF16)
    v_ref[...] = _dot(x, w_ref[:, 2 * d:3 * d]).astype(BF16)
    g_ref[...] = jax.nn.sigmoid(_dot(x, w_ref[:, 3 * d:4 * d])).astype(BF16)


def _hgrn_in(x, w_in_bf16, a_lb, layer, tm):
    t, d = x.shape
    row = pl.BlockSpec((tm, d), lambda i: (i, 0))
    return pl.pallas_call(
        functools.partial(_hgrn_in_kernel, layer=layer),
        grid=(t // tm,),
        in_specs=[row,
                  pl.BlockSpec((d, 4 * d), lambda i: (0, 0)),
                  pl.BlockSpec(a_lb.shape, lambda i: (0, 0))],
        out_specs=[row, row, row, row, row],
        out_shape=[jax.ShapeDtypeStruct((t, d), BF16)] * 4 + [jax.ShapeDtypeStruct((t, d), F32)],
        compiler_params=_cparams("parallel"),
        name="hgrn_in",
    )(x, w_in_bf16, a_lb)


def _hgrn_scan_kernel(q_ref, k_ref, v_ref, lf_ref, g_ref, ng_ref, o_ref, state_ref, *, n_chunks):
    c_len, dh = HGRN_CHUNK, HGRN_DH

    @pl.when(pl.program_id(1) == 0)
    def _():
        state_ref[...] = jnp.zeros_like(state_ref)

    row = lax.broadcasted_iota(I32, (c_len, c_len), 0)
    col = lax.broadcasted_iota(I32, (c_len, c_len), 1)
    causal = col <= row
    tri = jnp.where(causal, 1.0, 0.0).astype(BF16)

    heads = range(HGRN_HEADS)
    cols = [slice(h * dh, (h + 1) * dh) for h in heads]

    def chunk(c, carry):
        r0 = pl.multiple_of(c * c_len, c_len)
        rows = pl.ds(r0, c_len)
        lf_hi, lf_lo = _split_bf16(lf_ref[rows, :])
        b_all = _dot(tri, lf_hi) + _dot(tri, lf_lo)
        b = [b_all[:, cs] for cs in cols]
        q = [q_ref[rows, cs].astype(F32) for cs in cols]
        k = [k_ref[rows, cs].astype(F32) for cs in cols]
        v = [v_ref[rows, cs] for cs in cols]
        b_mid = [x[c_len // 2 - 1:c_len // 2, :] for x in b]
        b_last = [x[c_len - 1:c_len, :] for x in b]
        e_up = [jnp.exp(b[h] - b_mid[h]) for h in heads]
        e_dn = [jnp.exp(b_mid[h] - b[h]) for h in heads]
        qs = [q[h] * e_up[h] for h in heads]
        ks = [k[h] * e_dn[h] for h in heads]
        s = [jnp.where(causal, _dot_nt(qs[h].astype(BF16), ks[h].astype(BF16)), 0.0).astype(BF16)
             for h in heads]
        st = [state_ref[h] for h in heads]
        qd = [(qs[h] * jnp.exp(b_mid[h])).astype(BF16) for h in heads]
        o = [_dot(s[h], v[h]) + _dot_nt(qd[h], st[h].astype(BF16)) for h in heads]
        kd = [(ks[h] * jnp.exp(b_last[h] - b_mid[h])).astype(BF16) for h in heads]
        for h in heads:
            state_ref[h] = jnp.exp(b_last[h]) * st[h] + _dot_tn(v[h], kd[h])
        for h in heads:
            on = o[h] * lax.rsqrt(jnp.mean(o[h] * o[h], axis=-1, keepdims=True) + RMS_EPS)
            o_ref[rows, cols[h]] = (on * ng_ref[h:h + 1, :]
                                    * g_ref[rows, cols[h]].astype(F32)).astype(BF16)
        return carry

    lax.fori_loop(0, n_chunks, chunk, 0, unroll=2)


def _hgrn_scan(q, k, v, lf, g, norm_g, batch, seq, rows_per_step):
    t, d = q.shape
    steps = seq // rows_per_step
    blk = pl.BlockSpec((rows_per_step, d), lambda b, s: (b * steps + s, 0))
    return pl.pallas_call(
        functools.partial(_hgrn_scan_kernel, n_chunks=rows_per_step // HGRN_CHUNK),
        grid=(batch, steps),
        in_specs=[blk, blk, blk, blk, blk, pl.BlockSpec(norm_g.shape, lambda b, s: (0, 0))],
        out_specs=blk,
        out_shape=jax.ShapeDtypeStruct((t, d), BF16),
        scratch_shapes=[pltpu.VMEM((HGRN_HEADS, HGRN_DH, HGRN_DH), F32)],
        compiler_params=_cparams("parallel", "arbitrary"),
        name="hgrn_scan",
    )(q, k, v, lf, g, norm_g)


def _route_class(logits_t, rb):
    e = jnp.exp(logits_t - jnp.max(logits_t, axis=0, keepdims=True))
    sel = e / jnp.sum(e, axis=0, keepdims=True) + rb
    selr = [sel[i:i + 1, :] for i in range(N_EXPERTS)]
    n_per = EXPERTS_PER_GROUP

    def top2_sum(vals):
        best = None
        for i in range(len(vals)):
            for j in range(i + 1, len(vals)):
                pair = vals[i] + vals[j]
                best = pair if best is None else jnp.maximum(best, pair)
        return best

    gscore = [top2_sum(selr[g * n_per:(g + 1) * n_per]) for g in range(N_GROUPS)]
    gbest = jnp.zeros_like(gscore[0], dtype=I32)
    gval = gscore[0]
    for g in range(1, N_GROUPS):
        better = gscore[g] > gval
        gbest = jnp.where(better, g, gbest)
        gval = jnp.where(better, gscore[g], gval)

    def in_best_group(i):
        out = selr[(N_GROUPS - 1) * n_per + i]
        for g in range(N_GROUPS - 2, -1, -1):
            out = jnp.where(gbest == g, selr[g * n_per + i], out)
        return out

    sv = [in_best_group(i) for i in range(n_per)]
    i1 = jnp.zeros_like(gbest)
    v1 = sv[0]
    for i in range(1, n_per):
        better = sv[i] > v1
        i1 = jnp.where(better, i, i1)
        v1 = jnp.where(better, sv[i], v1)
    i2 = jnp.full_like(gbest, -1)
    v2 = jnp.full_like(v1, -jnp.inf)
    for i in range(n_per):
        better = (i1 != i) & (sv[i] > v2)
        i2 = jnp.where(better, i, i2)
        v2 = jnp.where(better, sv[i], v2)
    lo = jnp.minimum(i1, i2)
    hi = jnp.maximum(i1, i2)
    pair = jnp.where(lo == 0, hi - 1, jnp.where(lo == 1, hi + 1, PAIRS_PER_GROUP - 1))
    return gbest * PAIRS_PER_GROUP + pair


def _post_mix_kernel(x_ref, o_ref, w_ref, g_ref, b_ref, rw_ref, rb_ref, y_ref, cls_ref, *, sub):
    for r0 in range(0, x_ref.shape[0], sub):
        rows = slice(r0, r0 + sub)
        z = DEEPNORM_ALPHA * x_ref[rows, :] + _dot(o_ref[rows, :], w_ref[...])
        y = _layer_norm(z, g_ref[...], b_ref[...])
        y_ref[rows, :] = y
        yh, yl = _split_bf16(y)
        both = _dot(yh, rw_ref[...])
        logits = both[:, 0:LANES] + both[:, LANES:2 * LANES] + _dot(yl, rw_ref[:, 0:LANES])
        cls_ref[:, rows] = _route_class(jnp.transpose(logits)[0:N_EXPERTS, :], rb_ref[...])


def _post_mix(x, o, w_out_bf16, ln_g, ln_b, rw_cat, rb_col, tm):
    t, d = o.shape
    row = pl.BlockSpec((tm, d), lambda i: (i, 0))
    vec = pl.BlockSpec((1, d), lambda i: (0, 0))
    return pl.pallas_call(
        functools.partial(_post_mix_kernel, sub=min(256, tm)),
        grid=(t // tm,),
        in_specs=[row, row,
                  pl.BlockSpec((d, d), lambda i: (0, 0)),
                  vec, vec,
                  pl.BlockSpec((d, 2 * LANES), lambda i: (0, 0)),
                  pl.BlockSpec((N_EXPERTS, 1), lambda i: (0, 0))],
        out_specs=[row, pl.BlockSpec((1, tm), lambda i: (0, i))],
        out_shape=[jax.ShapeDtypeStruct((t, d), F32), jax.ShapeDtypeStruct((1, t), I32)],
        compiler_params=_cparams("parallel"),
        name="post_mix",
    )(x, o, w_out_bf16, ln_g, ln_b, rw_cat, rb_col)


def _plan_kernel(cls_ref, pos_ref, cnt_ref, cnt_s, base_s, run_s, *, tile):
    phase, i = pl.program_id(0), pl.program_id(1)
    tb = cls_ref.shape[1]
    onehot = cls_ref[...] == lax.broadcasted_iota(I32, (CLASS_ROWS, tb), 0)
    ohf = jnp.where(onehot, 1.0, 0.0)
    block_cnt = jnp.sum(ohf, axis=1, keepdims=True)

    @pl.when((phase == 0) & (i == 0))
    def _():
        cnt_s[...] = jnp.zeros_like(cnt_s)

    @pl.when(phase == 0)
    def _():
        cnt_s[...] += block_cnt

    @pl.when((phase == 1) & (i == 0))
    def _():
        cnt = cnt_s[...]
        ntile = jnp.floor((cnt + (tile - 1)) * (1.0 / tile))
        nt_hi = jnp.floor(ntile * (1.0 / 256.0))
        nt_lo = ntile - 256.0 * nt_hi
        r = lax.broadcasted_iota(I32, (CLASS_ROWS, CLASS_ROWS), 0)
        c = lax.broadcasted_iota(I32, (CLASS_ROWS, CLASS_ROWS), 1)
        below = jnp.where(c < r, 1.0, 0.0).astype(BF16)
        wide = (CLASS_ROWS, LANES)
        excl = (256.0 * _dot(below, jnp.broadcast_to(nt_hi, wide).astype(BF16))
                + _dot(below, jnp.broadcast_to(nt_lo, wide).astype(BF16)))
        base_s[...] = excl[:, 0:1] * tile
        run_s[...] = jnp.zeros_like(run_s)
        cnt_ref[...] = cnt

    @pl.when(phase == 1)
    def _():
        r = lax.broadcasted_iota(I32, (tb, tb), 0)
        c = lax.broadcasted_iota(I32, (tb, tb), 1)
        upto = jnp.where(r <= c, 1.0, 0.0).astype(BF16)
        within = _dot(ohf.astype(BF16), upto)
        rank = within + (run_s[...] + base_s[...] - 1.0)
        pos_ref[...] = jnp.sum(jnp.where(onehot, rank, 0.0), axis=0, keepdims=True).astype(I32)
        run_s[...] += block_cnt


def _plan(cls, tb, tile):
    t = cls.shape[1]
    return pl.pallas_call(
        functools.partial(_plan_kernel, tile=tile),
        grid=(2, t // tb),
        in_specs=[pl.BlockSpec((1, tb), lambda p, i: (0, i))],
        out_specs=[pl.BlockSpec((1, tb), lambda p, i: (0, i * p)),
                   pl.BlockSpec((CLASS_ROWS, 1), lambda p, i: (0, 0))],
        out_shape=[jax.ShapeDtypeStruct((1, t), I32), jax.ShapeDtypeStruct((CLASS_ROWS, 1), F32)],
        scratch_shapes=[pltpu.VMEM((CLASS_ROWS, 1), F32)] * 3,
        compiler_params=_cparams("arbitrary", "arbitrary"),
        name="moe_plan",
    )(cls)


def _invert_kernel(pos_ref, pad_lo_ref, pad_hi_ref, src_ref):
    def clear(i, carry):
        src_ref[i] = 0
        return carry

    batch = 16

    def place(b, carry):
        t0 = b * batch
        ps = [pos_ref[t0 + k] for k in range(batch)]
        for k in range(batch):
            src_ref[ps[k]] = t0 + k
        return carry

    for r in range(N_CLASSES + 1):
        lax.fori_loop(pad_lo_ref[r], pad_hi_ref[r], clear, 0)
    lax.fori_loop(0, pos_ref.shape[0] // batch, place, 0)


def _invert(pos, pad_lo, pad_hi, n_sorted):
    return pl.pallas_call(
        _invert_kernel,
        grid_spec=pltpu.PrefetchScalarGridSpec(
            num_scalar_prefetch=3, grid=(1,), in_specs=[],
            out_specs=pl.BlockSpec(memory_space=pltpu.SMEM)),
        out_shape=jax.ShapeDtypeStruct((n_sorted,), I32),
        compiler_params=_cparams("arbitrary"),
        name="moe_invert",
    )(pos, pad_lo, pad_hi)


def _start_row_gather(hbm, idx_ref, base, buf, slot, sem, lo, hi, rows=1):
    for j in range(lo, hi):
        src0 = idx_ref[base + j] * rows
        if rows > 1:
            src0 = pl.multiple_of(src0, rows)
        pltpu.make_async_copy(hbm.at[pl.ds(src0, rows)],
                              buf.at[slot, pl.ds(j * rows, rows)], sem).start(priority=j % 2)


def _row_groups(n, parts):
    edges = [n * p // parts for p in range(parts + 1)]
    return list(zip(edges[:-1], edges[1:]))


def _wait_row_gather(hbm, buf, slot, sem, n, rows=1):
    for j in range(n):
        pltpu.make_async_copy(hbm.at[pl.ds(0, rows)],
                              buf.at[slot, pl.ds(j * rows, rows)], sem).wait()


def _experts_kernel(ea_ref, eb_ref, nvalid_ref, src_ref, x_hbm, rwa_ref, rwb_ref,
                    wga_ref, wua_ref, wda_ref, wgb_ref, wub_ref, wdb_ref, ys_ref,
                    xbuf, xb_s, h_s, sems, *, tile):
    i = pl.program_id(0)
    nvalid = nvalid_ref[0]
    last = nvalid - 1

    @pl.when(i == 0)
    def _():
        _start_row_gather(x_hbm, src_ref, 0, xbuf, 0, sems.at[0], 0, tile)
        _start_row_gather(x_hbm, src_ref, jnp.minimum(1, last) * tile, xbuf, 1, sems.at[1], 0, tile)

    def run(slot):
        _wait_row_gather(x_hbm, xbuf, slot, sems.at[slot], tile)
        base = jnp.minimum(i + 2, last) * tile
        ahead = (slot + 2) % GATHER_SLOTS
        f, wn = D_EXPERT, MXU_COLS
        n_up, n_dn = f // wn, D_MODEL // wn
        groups = iter(_row_groups(tile, 1 + 2 * n_up + n_dn))

        def prefetch():
            _start_row_gather(x_hbm, src_ref, base, xbuf, ahead, sems.at[ahead], *next(groups))

        xf = xbuf[slot]
        dlogit = jnp.sum(xf * (rwa_ref[0] - rwb_ref[0]), axis=-1, keepdims=True)
        g_lo = 1.0 / (1.0 + jnp.exp(-dlogit))
        g_hi = 1.0 / (1.0 + jnp.exp(dlogit))
        xb_s[...] = xf.astype(BF16)
        prefetch()
        x = xb_s[...]
        for e, (wg_ref, wu_ref, gate) in enumerate(((wga_ref, wua_ref, g_lo), (wgb_ref, wub_ref, g_hi))):
            for n in range(n_up):
                cols = slice(n * wn, (n + 1) * wn)
                h = jax.nn.silu(_dot(x, wg_ref[0, :, cols])) * _dot(x, wu_ref[0, :, cols]) * gate
                h_s[:, e * f + n * wn:e * f + (n + 1) * wn] = h.astype(BF16)
                prefetch()
        for n in range(n_dn):
            cols = slice(n * wn, (n + 1) * wn)
            ys_ref[:, cols] = (_dot(h_s[:, 0:f], wda_ref[0, :, cols])
                               + _dot(h_s[:, f:2 * f], wdb_ref[0, :, cols]))
            prefetch()

    for s in range(GATHER_SLOTS):
        pl.when((i < nvalid) & (i % GATHER_SLOTS == s))(functools.partial(run, s))

    @pl.when(i == last)
    def _():
        for ahead in (1, 2):
            slot = (i + ahead) % GATHER_SLOTS
            _wait_row_gather(x_hbm, xbuf, slot, sems.at[slot], tile)

    @pl.when(i >= nvalid)
    def _():
        ys_ref[...] = jnp.zeros_like(ys_ref)


def _experts(ea, eb, nvalid, src, x, rw_rows, w_gate, w_up, w_down, tile):
    n_sorted = src.shape[0]
    d, f = D_MODEL, D_EXPERT
    rw_a = pl.BlockSpec((1, 1, d), lambda i, ea, eb, nv, src: (ea[i], 0, 0))
    rw_b = pl.BlockSpec((1, 1, d), lambda i, ea, eb, nv, src: (eb[i], 0, 0))
    up_a = pl.BlockSpec((1, d, f), lambda i, ea, eb, nv, src: (ea[i], 0, 0))
    up_b = pl.BlockSpec((1, d, f), lambda i, ea, eb, nv, src: (eb[i], 0, 0))
    dn_a = pl.BlockSpec((1, f, d), lambda i, ea, eb, nv, src: (ea[i], 0, 0))
    dn_b = pl.BlockSpec((1, f, d), lambda i, ea, eb, nv, src: (eb[i], 0, 0))
    return pl.pallas_call(
        functools.partial(_experts_kernel, tile=tile),
        grid_spec=pltpu.PrefetchScalarGridSpec(
            num_scalar_prefetch=4, grid=(n_sorted // tile,),
            in_specs=[pl.BlockSpec(memory_space=pl.ANY), rw_a, rw_b,
                      up_a, up_a, dn_a, up_b, up_b, dn_b],
            out_specs=pl.BlockSpec((tile, d), lambda i, ea, eb, nv, src: (i, 0)),
            scratch_shapes=[pltpu.VMEM((GATHER_SLOTS, tile, d), F32),
                            pltpu.VMEM((tile, d), BF16),
                            pltpu.VMEM((tile, 2 * f), BF16),
                            pltpu.SemaphoreType.DMA((GATHER_SLOTS,))]),
        out_shape=jax.ShapeDtypeStruct((n_sorted, d), F32),
        compiler_params=_cparams("arbitrary"),
        name="moe_experts",
    )(ea, eb, nvalid, src, x, rw_rows, rw_rows, w_gate, w_up, w_down, w_gate, w_up, w_down)


def _ln2_kernel(pos_ref, x_ref, ys_hbm, g_ref, b_ref, *rest, tile, with_qkv):
    if with_qkv:
        w_ref, o_ref, q_ref, k_ref, v_ref, ybuf, sems = rest
    else:
        o_ref, ybuf, sems = rest
    i = pl.program_id(0)
    last = pl.num_programs(0) - 1

    @pl.when(i == 0)
    def _():
        _start_row_gather(ys_hbm, pos_ref, 0, ybuf, 0, sems.at[0], 0, tile)
        _start_row_gather(ys_hbm, pos_ref, jnp.minimum(1, last) * tile, ybuf, 1, sems.at[1], 0, tile)

    def run(slot):
        _wait_row_gather(ys_hbm, ybuf, slot, sems.at[slot], tile)
        base = jnp.minimum(i + 2, last) * tile
        ahead = (slot + 2) % GATHER_SLOTS
        d, wn = D_MODEL, MXU_COLS
        groups = iter(_row_groups(tile, 1 + 3 * (d // wn) if with_qkv else 1))

        def prefetch():
            _start_row_gather(ys_hbm, pos_ref, base, ybuf, ahead, sems.at[ahead], *next(groups))

        z = DEEPNORM_ALPHA * x_ref[...] + ybuf[slot]
        prefetch()
        y = _layer_norm(z, g_ref[...], b_ref[...])
        o_ref[...] = y
        if with_qkv:
            yb = y.astype(BF16)
            for p, (out_ref, scale) in enumerate(((q_ref, DIFF_DH ** -0.5), (k_ref, None), (v_ref, None))):
                for n in range(d // wn):
                    cols = slice(n * wn, (n + 1) * wn)
                    r = _dot(yb, w_ref[:, p * d + n * wn:p * d + (n + 1) * wn])
                    out_ref[:, cols] = (r if scale is None else r * scale).astype(BF16)
                    prefetch()

    for s in range(GATHER_SLOTS):
        pl.when(i % GATHER_SLOTS == s)(functools.partial(run, s))

    @pl.when(i == last)
    def _():
        for ahead in (1, 2):
            slot = (i + ahead) % GATHER_SLOTS
            _wait_row_gather(ys_hbm, ybuf, slot, sems.at[slot], tile)


def _ln2(pos, x1, ys, ln_g, ln_b, w_qkv_bf16, tile):
    t, d = x1.shape
    with_qkv = w_qkv_bf16 is not None
    row = pl.BlockSpec((tile, d), lambda i, pos: (i, 0))
    vec = pl.BlockSpec((1, d), lambda i, pos: (0, 0))
    in_specs = [row, pl.BlockSpec(memory_space=pl.ANY), vec, vec]
    out_specs = [row]
    out_shape = [jax.ShapeDtypeStruct((t, d), F32)]
    args = [pos, x1, ys, ln_g, ln_b]
    if with_qkv:
        in_specs.append(pl.BlockSpec((d, 3 * d), lambda i, pos: (0, 0)))
        out_specs += [row, row, row]
        out_shape += [jax.ShapeDtypeStruct((t, d), BF16)] * 3
        args.append(w_qkv_bf16)
    return pl.pallas_call(
        functools.partial(_ln2_kernel, tile=tile, with_qkv=with_qkv),
        grid_spec=pltpu.PrefetchScalarGridSpec(
            num_scalar_prefetch=1, grid=(t // tile,),
            in_specs=in_specs, out_specs=out_specs,
            scratch_shapes=[pltpu.VMEM((GATHER_SLOTS, tile, d), F32), pltpu.SemaphoreType.DMA((GATHER_SLOTS,))]),
        out_shape=out_shape,
        compiler_params=_cparams("arbitrary"),
        name="ln2_qkv" if with_qkv else "ln2",
    )(*args)


def _moe_layer(x1, cls, rw_rows, w_gate, w_up, w_down, ln_g, ln_b, w_qkv_bf16):
    t = x1.shape[0]
    tile = min(MOE_TILE, t)
    n_tiles = t // tile + N_CLASSES
    pos, cnt = _plan(cls, min(1024, t), tile)
    pos = pos.reshape(t)
    cnt = cnt[:N_CLASSES, 0].astype(I32)
    ntile = (cnt + tile - 1) // tile
    tile_end = jnp.cumsum(ntile)
    nvalid = tile_end[-1]
    tid = jnp.minimum(jnp.arange(n_tiles, dtype=I32), nvalid - 1)
    tcls = jnp.sum((tid[:, None] >= tile_end[None, :]).astype(I32), axis=1)
    grp, pair = tcls // PAIRS_PER_GROUP, tcls % PAIRS_PER_GROUP
    ea = grp * EXPERTS_PER_GROUP + jnp.array([0, 0, 0, 1, 1, 2], I32)[pair]
    eb = grp * EXPERTS_PER_GROUP + jnp.array([1, 2, 3, 2, 3, 3], I32)[pair]
    base = (tile_end - ntile) * tile
    n_sorted = n_tiles * tile
    pad_lo = jnp.concatenate([base + cnt, (nvalid * tile).reshape(1)])
    pad_hi = jnp.concatenate([base + ntile * tile, jnp.full((1,), n_sorted, I32)])
    src = _invert(pos, pad_lo, pad_hi, n_sorted)
    ys = _experts(ea, eb, nvalid.reshape(1), src, x1, rw_rows, w_gate, w_up, w_down, tile)
    return _ln2(pos, x1, ys, ln_g, ln_b, w_qkv_bf16, min(LN2_TILE, t))


def _diff_attn_kernel(q_ref, k_ref, v_ref, lam_ref, sg_ref, o_ref, m_s, acc_s, *, tq, lambda_init):
    qi = pl.program_id(2)
    dh, dv = DIFF_DH, DIFF_DV
    heads = ATTN_HEADS_PER_STEP
    rows = 2 * tq
    lane = lax.broadcasted_iota(I32, (tq, dv), 1)
    qs = []
    for h in range(heads):
        q = q_ref[:, h * dv:(h + 1) * dv]
        zero = jnp.zeros_like(q)
        qs.append(jnp.concatenate([jnp.where(lane < dh, q, zero), jnp.where(lane >= dh, q, zero)],
                                  axis=0))
    m_s[...] = jnp.full_like(m_s, NEG_BIG)
    acc_s[...] = jnp.zeros_like(acc_s)
    ones = jnp.ones((1, dv), BF16)

    def block(r0, width, masked):
        ncol = width // LANES
        for h in range(heads):
            s = _dot_nt(qs[h], k_ref[pl.ds(r0, width), h * dv:(h + 1) * dv])
            if masked:
                row = lax.broadcasted_iota(I32, (tq, width), 0)
                col = lax.broadcasted_iota(I32, (tq, width), 1)
                keep = jnp.concatenate([col <= row, col <= row], axis=0)
                s = jnp.where(keep, s, NEG_BIG)
            cols = [s[:, c * LANES:(c + 1) * LANES] for c in range(ncol)]
            part = cols[0]
            for c in range(1, ncol):
                part = jnp.maximum(part, cols[c])
            m_old = m_s[h]
            m_new = jnp.maximum(m_old, jnp.broadcast_to(jnp.max(part, axis=-1, keepdims=True),
                                                        (rows, LANES)))
            p = jnp.concatenate([jnp.exp((cb - m_new).astype(BF16)) for cb in cols], axis=1)
            v = v_ref[pl.ds(r0, width), h * dv:(h + 1) * dv]
            v_ext = jnp.concatenate([v, jnp.broadcast_to(ones, (width, dv))], axis=1)
            scale = jnp.exp(m_old - m_new)
            acc_s[h] = acc_s[h] * jnp.concatenate([scale, scale], axis=1) + _dot(p, v_ext)
            m_s[h] = m_new

    wide = 2 * tq

    n_wide = qi // 2

    def body(j, carry):
        block(pl.multiple_of(2 * j * wide, wide), wide, False)
        block(pl.multiple_of((2 * j + 1) * wide, wide), wide, False)
        return carry

    lax.fori_loop(0, n_wide // 2, body, 0)

    @pl.when(n_wide % 2 == 1)
    def _():
        block(pl.multiple_of((n_wide - 1) * wide, wide), wide, False)

    @pl.when(qi % 2 == 1)
    def _():
        block(pl.multiple_of((qi - 1) * tq, tq), tq, False)

    block(pl.multiple_of(qi * tq, tq), tq, True)

    lam = lam_ref[...]
    lam_val = (jnp.exp(jnp.sum(lam[0:1] * lam[1:2], axis=1, keepdims=True))
               - jnp.exp(jnp.sum(lam[2:3] * lam[3:4], axis=1, keepdims=True)) + lambda_init)
    for h in range(heads):
        acc = acc_s[h]
        o12 = acc[:, 0:dv] / acc[:, dv:2 * dv]
        o = o12[0:tq] - lam_val * o12[tq:rows]
        o = o * lax.rsqrt(jnp.mean(o * o, axis=-1, keepdims=True) + RMS_EPS) * sg_ref[...]
        o_ref[:, h * dv:(h + 1) * dv] = (o * (1.0 - lambda_init)).astype(BF16)


def _diff_attn(q, k, v, lam, subln_g, batch, seq, tq, lambda_init):
    t, d = q.shape
    nq = seq // tq
    heads = ATTN_HEADS_PER_STEP
    w = heads * DIFF_DV
    return pl.pallas_call(
        functools.partial(_diff_attn_kernel, tq=tq, lambda_init=lambda_init),
        grid=(batch, DIFF_HEADS // heads, nq),
        in_specs=[pl.BlockSpec((tq, w), lambda b, h, i: (b * nq + i, h)),
                  pl.BlockSpec((seq, w), lambda b, h, i: (b, h)),
                  pl.BlockSpec((seq, w), lambda b, h, i: (b, h)),
                  pl.BlockSpec(lam.shape, lambda b, h, i: (0, 0)),
                  pl.BlockSpec((1, DIFF_DV), lambda b, h, i: (0, 0))],
        out_specs=pl.BlockSpec((tq, w), lambda b, h, i: (b * nq + i, h)),
        out_shape=jax.ShapeDtypeStruct((t, d), BF16),
        scratch_shapes=[pltpu.VMEM((heads, 2 * tq, LANES), F32),
                        pltpu.VMEM((heads, 2 * tq, 2 * DIFF_DV), F32)],
        compiler_params=_cparams("parallel", "parallel", "arbitrary"),
        name="diff_attn",
    )(q, k, v, lam, subln_g)


def kernel(x, a_w_in, a_lb, a_norm_g, a_w_out, kv_w, b_w_q, b_lam, b_subln_g, b_w_out,
           ln1_g, ln1_b, ln2_g, ln2_b, router_w, router_b, moe_w_gate, moe_w_up, moe_w_down):
    batch, seq, d = x.shape
    t = batch * seq
    tm = min(512, t)
    xt = x.reshape(t, d)

    rw = jnp.pad(router_w, ((0, 0), (0, LANES - N_EXPERTS)))
    rw_hi = rw.astype(BF16)
    rw_cat = jnp.concatenate([rw_hi, (rw - rw_hi.astype(F32)).astype(BF16)], axis=1)
    rb_col = router_b.reshape(N_EXPERTS, 1)
    rw_rows = router_w.T.reshape(N_EXPERTS, 1, d)
    w_qkv = jnp.concatenate([b_w_q[0], kv_w], axis=1).astype(BF16)

    qkv = None
    for layer in range(DEPTH):
        if layer < N_A_LAYERS:
            q, k, v, g, lf = _hgrn_in(xt, a_w_in[layer].astype(BF16), a_lb, layer, tm)
            o = _hgrn_scan(q, k, v, lf, g, a_norm_g[layer], batch, seq, min(512, seq))
            w_out = a_w_out[layer]
        else:
            j = layer - N_A_LAYERS
            lambda_init = 0.8 - 0.6 * math.exp(-0.3 * layer)
            o = _diff_attn(*qkv, b_lam[j], b_subln_g[j].reshape(1, DIFF_DV), batch, seq,
                           min(ATTN_TQ, seq), lambda_init)
            w_out = b_w_out[j]
        x1, cls = _post_mix(xt, o, w_out.astype(BF16), ln1_g[layer].reshape(1, d),
                            ln1_b[layer].reshape(1, d), rw_cat, rb_col, tm)
        outs = _moe_layer(x1, cls, rw_rows, moe_w_gate[layer].astype(BF16),
                          moe_w_up[layer].astype(BF16), moe_w_down[layer].astype(BF16),
                          ln2_g[layer].reshape(1, d), ln2_b[layer].reshape(1, d),
                          w_qkv if layer + 1 == N_A_LAYERS else None)
        xt, qkv = outs[0], outs[1:]
    return xt.reshape(batch, seq, d)
```

```python
import functools
import math

import jax
import jax.numpy as jnp
from jax import lax
from jax.experimental import pallas as pl
from jax.experimental.pallas import tpu as pltpu

F32 = jnp.float32
BF16 = jnp.bfloat16
I32 = jnp.int32

D_MODEL = 1024
DEPTH = 2
N_A_LAYERS = DEPTH // 2
HGRN_HEADS = 8
HGRN_DH = D_MODEL // HGRN_HEADS
HGRN_CHUNK = 64
DIFF_HEADS = 8
DIFF_DH = D_MODEL // (2 * DIFF_HEADS)
DIFF_DV = 2 * DIFF_DH
N_EXPERTS = 16
N_GROUPS = 4
EXPERTS_PER_GROUP = N_EXPERTS // N_GROUPS
PAIRS_PER_GROUP = 6
N_CLASSES = N_GROUPS * PAIRS_PER_GROUP
CLASS_ROWS = 32
D_EXPERT = 512
DEEPNORM_ALPHA = (2.0 * DEPTH) ** 0.25
LN_EPS = 1e-5
RMS_EPS = 1e-6

LANES = 128
MXU_COLS = 256
MOE_TILE = 256
LN2_TILE = 256
GATHER_SLOTS = 3
ATTN_TQ = 256
ATTN_TK = 512
ATTN_HEADS_PER_STEP = 8
NEG_BIG = -1e30
VMEM_LIMIT = 56 * 1024 * 1024


def _cparams(*sem):
    return pltpu.CompilerParams(dimension_semantics=sem, vmem_limit_bytes=VMEM_LIMIT)


def _dot(a, b):
    return jnp.dot(a, b, preferred_element_type=F32)


def _dot_nt(a, b):
    return lax.dot_general(a, b, (((1,), (1,)), ((), ())), preferred_element_type=F32)


def _dot_tn(a, b):
    return lax.dot_general(a, b, (((0,), (0,)), ((), ())), preferred_element_type=F32)


def _split_bf16(x):
    hi = x.astype(BF16)
    lo = (x - hi.astype(F32)).astype(BF16)
    return hi, lo


def _layer_norm(z, g, b):
    mu = jnp.mean(z, axis=-1, keepdims=True)
    zc = z - mu
    var = jnp.mean(zc * zc, axis=-1, keepdims=True)
    return zc * lax.rsqrt(var + LN_EPS) * g + b


def _hgrn_in_kernel(x_ref, w_ref, lb_ref, q_ref, k_ref, v_ref, g_ref, lf_ref, *, layer):
    d = D_MODEL
    x = x_ref[...].astype(BF16)
    a = lb_ref[...]
    e = jnp.exp(a - jnp.max(a, axis=0, keepdims=True))
    sm = e / jnp.sum(e, axis=0, keepdims=True)
    lb = jnp.sum(sm[0:layer + 1], axis=0, keepdims=True)
    q_ref[...] = _dot(x, w_ref[:, 0:d]).astype(BF16)
    f = lb + (1.0 - lb) * jax.nn.sigmoid(_dot(x, w_ref[:, d:2 * d]))
    lf_ref[...] = jnp.log(f)
    k_ref[...] = (1.0 - f).astype(BF16)
    v_ref[...] = _dot(x, w_ref[:, 2 * d:3 * d]).astype(BF16)
    g_ref[...] = jax.nn.sigmoid(_dot(x, w_ref[:, 3 * d:4 * d])).astype(BF16)


def _hgrn_in(x, w_in_bf16, a_lb, layer, tm):
    t, d = x.shape
    row = pl.BlockSpec((tm, d), lambda i: (i, 0))
    return pl.pallas_call(
        functools.partial(_hgrn_in_kernel, layer=layer),
        grid=(t // tm,),
        in_specs=[row,
                  pl.BlockSpec((d, 4 * d), lambda i: (0, 0)),
                  pl.BlockSpec(a_lb.shape, lambda i: (0, 0))],
        out_specs=[row, row, row, row, row],
        out_shape=[jax.ShapeDtypeStruct((t, d), BF16)] * 4 + [jax.ShapeDtypeStruct((t, d), F32)],
        compiler_params=_cparams("parallel"),
        name="hgrn_in",
    )(x, w_in_bf16, a_lb)


def _hgrn_scan_kernel(q_ref, k_ref, v_ref, lf_ref, g_ref, ng_ref, o_ref, state_ref, *, n_chunks):
    c_len, dh = HGRN_CHUNK, HGRN_DH

    @pl.when(pl.program_id(1) == 0)
    def _():
        state_ref[...] = jnp.zeros_like(state_ref)

    row = lax.broadcasted_iota(I32, (c_len, c_len), 0)
    col = lax.broadcasted_iota(I32, (c_len, c_len), 1)
    causal = col <= row
    tri = jnp.where(causal, 1.0, 0.0).astype(BF16)

    heads = range(HGRN_HEADS)
    cols = [slice(h * dh, (h + 1) * dh) for h in heads]

    def chunk(c, carry):
        r0 = pl.multiple_of(c * c_len, c_len)
        rows = pl.ds(r0, c_len)
        lf_hi, lf_lo = _split_bf16(lf_ref[rows, :])
        b_all = _dot(tri, lf_hi) + _dot(tri, lf_lo)
        b = [b_all[:, cs] for cs in cols]
        q = [q_ref[rows, cs].astype(F32) for cs in cols]
        k = [k_ref[rows, cs].astype(F32) for cs in cols]
        v = [v_ref[rows, cs] for cs in cols]
        b_mid = [x[c_len // 2 - 1:c_len // 2, :] for x in b]
        b_last = [x[c_len - 1:c_len, :] for x in b]
        e_up = [jnp.exp(b[h] - b_mid[h]) for h in heads]
        e_dn = [jnp.exp(b_mid[h] - b[h]) for h in heads]
        qs = [q[h] * e_up[h] for h in heads]
        ks = [k[h] * e_dn[h] for h in heads]
        s = [jnp.where(causal, _dot_nt(qs[h].astype(BF16), ks[h].astype(BF16)), 0.0).astype(BF16)
             for h in heads]
        st = [state_ref[h] for h in heads]
        qd = [(qs[h] * jnp.exp(b_mid[h])).astype(BF16) for h in heads]
        o = [_dot(s[h], v[h]) + _dot_nt(qd[h], st[h].astype(BF16)) for h in heads]
        kd = [(ks[h] * jnp.exp(b_last[h] - b_mid[h])).astype(BF16) for h in heads]
        for h in heads:
            state_ref[h] = jnp.exp(b_last[h]) * st[h] + _dot_tn(v[h], kd[h])
        for h in heads:
            on = o[h] * lax.rsqrt(jnp.mean(o[h] * o[h], axis=-1, keepdims=True) + RMS_EPS)
            o_ref[rows, cols[h]] = (on * ng_ref[h:h + 1, :]
                                    * g_ref[rows, cols[h]].astype(F32)).astype(BF16)
        return carry

    lax.fori_loop(0, n_chunks, chunk, 0, unroll=2)


def _hgrn_scan(q, k, v, lf, g, norm_g, batch, seq, rows_per_step):
    t, d = q.shape
    steps = seq // rows_per_step
    blk = pl.BlockSpec((rows_per_step, d), lambda b, s: (b * steps + s, 0))
    return pl.pallas_call(
        functools.partial(_hgrn_scan_kernel, n_chunks=rows_per_step // HGRN_CHUNK),
        grid=(batch, steps),
        in_specs=[blk, blk, blk, blk, blk, pl.BlockSpec(norm_g.shape, lambda b, s: (0, 0))],
        out_specs=blk,
        out_shape=jax.ShapeDtypeStruct((t, d), BF16),
        scratch_shapes=[pltpu.VMEM((HGRN_HEADS, HGRN_DH, HGRN_DH), F32)],
        compiler_params=_cparams("parallel", "arbitrary"),
        name="hgrn_scan",
    )(q, k, v, lf, g, norm_g)


def _route_class(logits_t, rb):
    e = jnp.exp(logits_t - jnp.max(logits_t, axis=0, keepdims=True))
    sel = e / jnp.sum(e, axis=0, keepdims=True) + rb
    selr = [sel[i:i + 1, :] for i in range(N_EXPERTS)]
    n_per = EXPERTS_PER_GROUP

    def top2_sum(vals):
        best = None
        for i in range(len(vals)):
            for j in range(i + 1, len(vals)):
                pair = vals[i] + vals[j]
                best = pair if best is None else jnp.maximum(best, pair)
        return best

    gscore = [top2_sum(selr[g * n_per:(g + 1) * n_per]) for g in range(N_GROUPS)]
    gbest = jnp.zeros_like(gscore[0], dtype=I32)
    gval = gscore[0]
    for g in range(1, N_GROUPS):
        better = gscore[g] > gval
        gbest = jnp.where(better, g, gbest)
        gval = jnp.where(better, gscore[g], gval)

    def in_best_group(i):
        out = selr[(N_GROUPS - 1) * n_per + i]
        for g in range(N_GROUPS - 2, -1, -1):
            out = jnp.where(gbest == g, selr[g * n_per + i], out)
        return out

    sv = [in_best_group(i) for i in range(n_per)]
    i1 = jnp.zeros_like(gbest)
    v1 = sv[0]
    for i in range(1, n_per):
        better = sv[i] > v1
        i1 = jnp.where(better, i, i1)
        v1 = jnp.where(better, sv[i], v1)
    i2 = jnp.full_like(gbest, -1)
    v2 = jnp.full_like(v1, -jnp.inf)
    for i in range(n_per):
        better = (i1 != i) & (sv[i] > v2)
        i2 = jnp.where(better, i, i2)
        v2 = jnp.where(better, sv[i], v2)
    lo = jnp.minimum(i1, i2)
    hi = jnp.maximum(i1, i2)
    pair = jnp.where(lo == 0, hi - 1, jnp.where(lo == 1, hi + 1, PAIRS_PER_GROUP - 1))
    return gbest * PAIRS_PER_GROUP + pair


def _post_mix_kernel(x_ref, o_ref, w_ref, g_ref, b_ref, rw_ref, rb_ref, y_ref, cls_ref, *, sub):
    for r0 in range(0, x_ref.shape[0], sub):
        rows = slice(r0, r0 + sub)
        z = DEEPNORM_ALPHA * x_ref[rows, :] + _dot(o_ref[rows, :], w_ref[...])
        y = _layer_norm(z, g_ref[...], b_ref[...])
        y_ref[rows, :] = y
        yh, yl = _split_bf16(y)
        both = _dot(yh, rw_ref[...])
        logits = both[:, 0:LANES] + both[:, LANES:2 * LANES] + _dot(yl, rw_ref[:, 0:LANES])
        cls_ref[:, rows] = _route_class(jnp.transpose(logits)[0:N_EXPERTS, :], rb_ref[...])


def _post_mix(x, o, w_out_bf16, ln_g, ln_b, rw_cat, rb_col, tm):
    t, d = o.shape
    row = pl.BlockSpec((tm, d), lambda i: (i, 0))
    vec = pl.BlockSpec((1, d), lambda i: (0, 0))
    return pl.pallas_call(
        functools.partial(_post_mix_kernel, sub=min(256, tm)),
        grid=(t // tm,),
        in_specs=[row, row,
                  pl.BlockSpec((d, d), lambda i: (0, 0)),
                  vec, vec,
                  pl.BlockSpec((d, 2 * LANES), lambda i: (0, 0)),
                  pl.BlockSpec((N_EXPERTS, 1), lambda i: (0, 0))],
        out_specs=[row, pl.BlockSpec((1, tm), lambda i: (0, i))],
        out_shape=[jax.ShapeDtypeStruct((t, d), F32), jax.ShapeDtypeStruct((1, t), I32)],
        compiler_params=_cparams("parallel"),
        name="post_mix",
    )(x, o, w_out_bf16, ln_g, ln_b, rw_cat, rb_col)


def _plan_kernel(cls_ref, pos_ref, cnt_ref, cnt_s, base_s, run_s, *, tile):
    phase, i = pl.program_id(0), pl.program_id(1)
    tb = cls_ref.shape[1]
    onehot = cls_ref[...] == lax.broadcasted_iota(I32, (CLASS_ROWS, tb), 0)
    ohf = jnp.where(onehot, 1.0, 0.0)
    block_cnt = jnp.sum(ohf, axis=1, keepdims=True)

    @pl.when((phase == 0) & (i == 0))
    def _():
        cnt_s[...] = jnp.zeros_like(cnt_s)

    @pl.when(phase == 0)
    def _():
        cnt_s[...] += block_cnt

    @pl.when((phase == 1) & (i == 0))
    def _():
        cnt = cnt_s[...]
        ntile = jnp.floor((cnt + (tile - 1)) * (1.0 / tile))
        nt_hi = jnp.floor(ntile * (1.0 / 256.0))
        nt_lo = ntile - 256.0 * nt_hi
        r = lax.broadcasted_iota(I32, (CLASS_ROWS, CLASS_ROWS), 0)
        c = lax.broadcasted_iota(I32, (CLASS_ROWS, CLASS_ROWS), 1)
        below = jnp.where(c < r, 1.0, 0.0).astype(BF16)
        wide = (CLASS_ROWS, LANES)
        excl = (256.0 * _dot(below, jnp.broadcast_to(nt_hi, wide).astype(BF16))
                + _dot(below, jnp.broadcast_to(nt_lo, wide).astype(BF16)))
        base_s[...] = excl[:, 0:1] * tile
        run_s[...] = jnp.zeros_like(run_s)
        cnt_ref[...] = cnt

    @pl.when(phase == 1)
    def _():
        r = lax.broadcasted_iota(I32, (tb, tb), 0)
        c = lax.broadcasted_iota(I32, (tb, tb), 1)
        upto = jnp.where(r <= c, 1.0, 0.0).astype(BF16)
        within = _dot(ohf.astype(BF16), upto)
        rank = within + (run_s[...] + base_s[...] - 1.0)
        pos_ref[...] = jnp.sum(jnp.where(onehot, rank, 0.0), axis=0, keepdims=True).astype(I32)
        run_s[...] += block_cnt


def _plan(cls, tb, tile):
    t = cls.shape[1]
    return pl.pallas_call(
        functools.partial(_plan_kernel, tile=tile),
        grid=(2, t // tb),
        in_specs=[pl.BlockSpec((1, tb), lambda p, i: (0, i))],
        out_specs=[pl.BlockSpec((1, tb), lambda p, i: (0, i * p)),
                   pl.BlockSpec((CLASS_ROWS, 1), lambda p, i: (0, 0))],
        out_shape=[jax.ShapeDtypeStruct((1, t), I32), jax.ShapeDtypeStruct((CLASS_ROWS, 1), F32)],
        scratch_shapes=[pltpu.VMEM((CLASS_ROWS, 1), F32)] * 3,
        compiler_params=_cparams("arbitrary", "arbitrary"),
        name="moe_plan",
    )(cls)


def _invert_kernel(pos_ref, pad_lo_ref, pad_hi_ref, src_ref):
    def clear(i, carry):
        src_ref[i] = 0
        return carry

    batch = 16

    def place(b, carry):
        t0 = b * batch
        ps = [pos_ref[t0 + k] for k in range(batch)]
        for k in range(batch):
            src_ref[ps[k]] = t0 + k
        return carry

    for r in range(N_CLASSES + 1):
        lax.fori_loop(pad_lo_ref[r], pad_hi_ref[r], clear, 0)
    lax.fori_loop(0, pos_ref.shape[0] // batch, place, 0)


def _invert(pos, pad_lo, pad_hi, n_sorted):
    return pl.pallas_call(
        _invert_kernel,
        grid_spec=pltpu.PrefetchScalarGridSpec(
            num_scalar_prefetch=3, grid=(1,), in_specs=[],
            out_specs=pl.BlockSpec(memory_space=pltpu.SMEM)),
        out_shape=jax.ShapeDtypeStruct((n_sorted,), I32),
        compiler_params=_cparams("arbitrary"),
        name="moe_invert",
    )(pos, pad_lo, pad_hi)


def _start_row_gather(hbm, idx_ref, base, buf, slot, sem, lo, hi, rows=1):
    for j in range(lo, hi):
        src0 = idx_ref[base + j] * rows
        if rows > 1:
            src0 = pl.multiple_of(src0, rows)
        pltpu.make_async_copy(hbm.at[pl.ds(src0, rows)],
                              buf.at[slot, pl.ds(j * rows, rows)], sem).start(priority=j % 2)


def _row_groups(n, parts):
    edges = [n * p // parts for p in range(parts + 1)]
    return list(zip(edges[:-1], edges[1:]))


def _wait_row_gather(hbm, buf, slot, sem, n, rows=1):
    for j in range(n):
        pltpu.make_async_copy(hbm.at[pl.ds(0, rows)],
                              buf.at[slot, pl.ds(j * rows, rows)], sem).wait()


def _experts_kernel(ea_ref, eb_ref, nvalid_ref, src_ref, x_hbm, rwa_ref, rwb_ref,
                    wga_ref, wua_ref, wda_ref, wgb_ref, wub_ref, wdb_ref, ys_ref,
                    xbuf, xb_s, h_s, wga_s, wua_s, wda_s, wgb_s, wub_s, wdb_s, sems, *, tile):
    i = pl.program_id(0)
    nvalid = nvalid_ref[0]
    last = nvalid - 1

    prev = jnp.maximum(i - 1, 0)
    for e_ref, pairs in ((ea_ref, ((wga_ref, wga_s), (wua_ref, wua_s), (wda_ref, wda_s))),
                         (eb_ref, ((wgb_ref, wgb_s), (wub_ref, wub_s), (wdb_ref, wdb_s)))):
        @pl.when((i < nvalid) & ((i == 0) | (e_ref[i] != e_ref[prev])))
        def _(pairs=pairs):
            for w_ref, w_s in pairs:
                w_s[...] = w_ref[0, 0].astype(BF16)

    @pl.when(i == 0)
    def _():
        _start_row_gather(x_hbm, src_ref, 0, xbuf, 0, sems.at[0], 0, tile)
        _start_row_gather(x_hbm, src_ref, jnp.minimum(1, last) * tile, xbuf, 1, sems.at[1], 0, tile)

    def run(slot):
        _wait_row_gather(x_hbm, xbuf, slot, sems.at[slot], tile)
        base = jnp.minimum(i + 2, last) * tile
        ahead = (slot + 2) % GATHER_SLOTS
        f, wn = D_EXPERT, MXU_COLS
        n_up, n_dn = f // wn, D_MODEL // wn
        groups = iter(_row_groups(tile, 1 + 2 * n_up + n_dn))

        def prefetch():
            _start_row_gather(x_hbm, src_ref, base, xbuf, ahead, sems.at[ahead], *next(groups))

        xf = xbuf[slot]
        dlogit = jnp.sum(xf * (rwa_ref[0] - rwb_ref[0]), axis=-1, keepdims=True)
        g_lo = 1.0 / (1.0 + jnp.exp(-dlogit))
        g_hi = 1.0 / (1.0 + jnp.exp(dlogit))
        xb_s[...] = xf.astype(BF16)
        prefetch()
        x = xb_s[...]
        for e, (wg_s, wu_s, gate) in enumerate(((wga_s, wua_s, g_lo), (wgb_s, wub_s, g_hi))):
            for n in range(n_up):
                cols = slice(n * wn, (n + 1) * wn)
                h = jax.nn.silu(_dot(x, wg_s[:, cols])) * _dot(x, wu_s[:, cols]) * gate
                h_s[:, e * f + n * wn:e * f + (n + 1) * wn] = h.astype(BF16)
                prefetch()
        for n in range(n_dn):
            cols = slice(n * wn, (n + 1) * wn)
            ys_ref[:, cols] = (_dot(h_s[:, 0:f], wda_s[:, cols])
                               + _dot(h_s[:, f:2 * f], wdb_s[:, cols]))
            prefetch()

    for s in range(GATHER_SLOTS):
        pl.when((i < nvalid) & (i % GATHER_SLOTS == s))(functools.partial(run, s))

    @pl.when(i == last)
    def _():
        for ahead in (1, 2):
            slot = (i + ahead) % GATHER_SLOTS
            _wait_row_gather(x_hbm, xbuf, slot, sems.at[slot], tile)

    @pl.when(i >= nvalid)
    def _():
        ys_ref[...] = jnp.zeros_like(ys_ref)


def _experts(ea, eb, nvalid, src, x, rw_rows, w_gate, w_up, w_down, layer, tile):
    n_sorted = src.shape[0]
    d, f = D_MODEL, D_EXPERT
    rw_a = pl.BlockSpec((1, 1, d), lambda i, ea, eb, nv, src: (ea[i], 0, 0))
    rw_b = pl.BlockSpec((1, 1, d), lambda i, ea, eb, nv, src: (eb[i], 0, 0))
    up_a = pl.BlockSpec((1, 1, d, f), lambda i, ea, eb, nv, src: (layer, ea[i], 0, 0))
    up_b = pl.BlockSpec((1, 1, d, f), lambda i, ea, eb, nv, src: (layer, eb[i], 0, 0))
    dn_a = pl.BlockSpec((1, 1, f, d), lambda i, ea, eb, nv, src: (layer, ea[i], 0, 0))
    dn_b = pl.BlockSpec((1, 1, f, d), lambda i, ea, eb, nv, src: (layer, eb[i], 0, 0))
    return pl.pallas_call(
        functools.partial(_experts_kernel, tile=tile),
        grid_spec=pltpu.PrefetchScalarGridSpec(
            num_scalar_prefetch=4, grid=(n_sorted // tile,),
            in_specs=[pl.BlockSpec(memory_space=pl.ANY), rw_a, rw_b,
                      up_a, up_a, dn_a, up_b, up_b, dn_b],
            out_specs=pl.BlockSpec((tile, d), lambda i, ea, eb, nv, src: (i, 0)),
            scratch_shapes=[pltpu.VMEM((GATHER_SLOTS, tile, d), F32),
                            pltpu.VMEM((tile, d), BF16),
                            pltpu.VMEM((tile, 2 * f), BF16),
                            pltpu.VMEM((d, f), BF16), pltpu.VMEM((d, f), BF16),
                            pltpu.VMEM((f, d), BF16),
                            pltpu.VMEM((d, f), BF16), pltpu.VMEM((d, f), BF16),
                            pltpu.VMEM((f, d), BF16),
                            pltpu.SemaphoreType.DMA((GATHER_SLOTS,))]),
        out_shape=jax.ShapeDtypeStruct((n_sorted, d), F32),
        compiler_params=_cparams("arbitrary"),
        name="moe_experts",
    )(ea, eb, nvalid, src, x, rw_rows, rw_rows, w_gate, w_up, w_down, w_gate, w_up, w_down)


def _ln2_kernel(pos_ref, x_ref, ys_hbm, g_ref, b_ref, *rest, tile, with_qkv):
    if with_qkv:
        w_ref, o_ref, q_ref, k_ref, v_ref, ybuf, sems = rest
    else:
        o_ref, ybuf, sems = rest
    i = pl.program_id(0)
    last = pl.num_programs(0) - 1

    @pl.when(i == 0)
    def _():
        _start_row_gather(ys_hbm, pos_ref, 0, ybuf, 0, sems.at[0], 0, tile)
        _start_row_gather(ys_hbm, pos_ref, jnp.minimum(1, last) * tile, ybuf, 1, sems.at[1], 0, tile)

    def run(slot):
        _wait_row_gather(ys_hbm, ybuf, slot, sems.at[slot], tile)
        base = jnp.minimum(i + 2, last) * tile
        ahead = (slot + 2) % GATHER_SLOTS
        d, wn = D_MODEL, MXU_COLS
        groups = iter(_row_groups(tile, 1 + 3 * (d // wn) if with_qkv else 1))

        def prefetch():
            _start_row_gather(ys_hbm, pos_ref, base, ybuf, ahead, sems.at[ahead], *next(groups))

        z = DEEPNORM_ALPHA * x_ref[...] + ybuf[slot]
        prefetch()
        y = _layer_norm(z, g_ref[...], b_ref[...])
        o_ref[...] = y
        if with_qkv:
            yb = y.astype(BF16)
            for p, (out_ref, scale) in enumerate(((q_ref, DIFF_DH ** -0.5), (k_ref, None), (v_ref, None))):
                for n in range(d // wn):
                    cols = slice(n * wn, (n + 1) * wn)
                    r = _dot(yb, w_ref[:, p * d + n * wn:p * d + (n + 1) * wn])
                    out_ref[:, cols] = (r if scale is None else r * scale).astype(BF16)
                    prefetch()

    for s in range(GATHER_SLOTS):
        pl.when(i % GATHER_SLOTS == s)(functools.partial(run, s))

    @pl.when(i == last)
    def _():
        for ahead in (1, 2):
            slot = (i + ahead) % GATHER_SLOTS
            _wait_row_gather(ys_hbm, ybuf, slot, sems.at[slot], tile)


def _ln2(pos, x1, ys, ln_g, ln_b, w_qkv_bf16, tile):
    t, d = x1.shape
    with_qkv = w_qkv_bf16 is not None
    row = pl.BlockSpec((tile, d), lambda i, pos: (i, 0))
    vec = pl.BlockSpec((1, d), lambda i, pos: (0, 0))
    in_specs = [row, pl.BlockSpec(memory_space=pl.ANY), vec, vec]
    out_specs = [row]
    out_shape = [jax.ShapeDtypeStruct((t, d), F32)]
    args = [pos, x1, ys, ln_g, ln_b]
    if with_qkv:
        in_specs.append(pl.BlockSpec((d, 3 * d), lambda i, pos: (0, 0)))
        out_specs += [row, row, row]
        out_shape += [jax.ShapeDtypeStruct((t, d), BF16)] * 3
        args.append(w_qkv_bf16)
    return pl.pallas_call(
        functools.partial(_ln2_kernel, tile=tile, with_qkv=with_qkv),
        grid_spec=pltpu.PrefetchScalarGridSpec(
            num_scalar_prefetch=1, grid=(t // tile,),
            in_specs=in_specs, out_specs=out_specs,
            scratch_shapes=[pltpu.VMEM((GATHER_SLOTS, tile, d), F32), pltpu.SemaphoreType.DMA((GATHER_SLOTS,))]),
        out_shape=out_shape,
        compiler_params=_cparams("arbitrary"),
        name="ln2_qkv" if with_qkv else "ln2",
    )(*args)


def _moe_layer(x1, cls, rw_rows, w_gate, w_up, w_down, layer, ln_g, ln_b, w_qkv_bf16):
    t = x1.shape[0]
    tile = min(MOE_TILE, t)
    n_tiles = t // tile + N_CLASSES
    pos, cnt = _plan(cls, min(1024, t), tile)
    pos = pos.reshape(t)
    cnt = cnt[:N_CLASSES, 0].astype(I32)
    ntile = (cnt + tile - 1) // tile
    tile_end = jnp.cumsum(ntile)
    nvalid = tile_end[-1]
    tid = jnp.minimum(jnp.arange(n_tiles, dtype=I32), nvalid - 1)
    tcls = jnp.sum((tid[:, None] >= tile_end[None, :]).astype(I32), axis=1)
    grp, pair = tcls // PAIRS_PER_GROUP, tcls % PAIRS_PER_GROUP
    ea = grp * EXPERTS_PER_GROUP + jnp.array([0, 0, 0, 1, 1, 2], I32)[pair]
    eb = grp * EXPERTS_PER_GROUP + jnp.array([1, 2, 3, 2, 3, 3], I32)[pair]
    base = (tile_end - ntile) * tile
    n_sorted = n_tiles * tile
    pad_lo = jnp.concatenate([base + cnt, (nvalid * tile).reshape(1)])
    pad_hi = jnp.concatenate([base + ntile * tile, jnp.full((1,), n_sorted, I32)])
    src = _invert(pos, pad_lo, pad_hi, n_sorted)
    ys = _experts(ea, eb, nvalid.reshape(1), src, x1, rw_rows, w_gate, w_up, w_down, layer,
                  tile)
    return _ln2(pos, x1, ys, ln_g, ln_b, w_qkv_bf16, min(LN2_TILE, t))


def _diff_attn_kernel(q_ref, k_ref, v_ref, lam_ref, sg_ref, o_ref, m_s, acc_s, *, tq, lambda_init):
    qi = pl.program_id(2)
    dh, dv = DIFF_DH, DIFF_DV
    heads = ATTN_HEADS_PER_STEP
    rows = 2 * tq
    lane = lax.broadcasted_iota(I32, (tq, dv), 1)
    qs = []
    for h in range(heads):
        q = q_ref[:, h * dv:(h + 1) * dv]
        zero = jnp.zeros_like(q)
        qs.append(jnp.concatenate([jnp.where(lane < dh, q, zero), jnp.where(lane >= dh, q, zero)],
                                  axis=0))
    m_s[...] = jnp.full_like(m_s, NEG_BIG)
    acc_s[...] = jnp.zeros_like(acc_s)
    ones = jnp.ones((1, dv), BF16)

    def block(r0, width, masked):
        ncol = width // LANES
        for h in range(heads):
            s = _dot_nt(qs[h], k_ref[pl.ds(r0, width), h * dv:(h + 1) * dv])
            if masked:
                row = lax.broadcasted_iota(I32, (tq, width), 0)
                col = lax.broadcasted_iota(I32, (tq, width), 1)
                keep = jnp.concatenate([col <= row, col <= row], axis=0)
                s = jnp.where(keep, s, NEG_BIG)
            cols = [s[:, c * LANES:(c + 1) * LANES] for c in range(ncol)]
            part = cols[0]
            for c in range(1, ncol):
                part = jnp.maximum(part, cols[c])
            m_old = m_s[h]
            m_new = jnp.maximum(m_old, jnp.broadcast_to(jnp.max(part, axis=-1, keepdims=True),
                                                        (rows, LANES)))
            p = jnp.concatenate([jnp.exp((cb - m_new).astype(BF16)) for cb in cols], axis=1)
            v = v_ref[pl.ds(r0, width), h * dv:(h + 1) * dv]
            v_ext = jnp.concatenate([v, jnp.broadcast_to(ones, (width, dv))], axis=1)
            scale = jnp.exp(m_old - m_new)
            acc_s[h] = acc_s[h] * jnp.concatenate([scale, scale], axis=1) + _dot(p, v_ext)
            m_s[h] = m_new

    wide = max(tq, min(ATTN_TK, k_ref.shape[0]))
    per_wide = wide // tq
    n_wide = qi // per_wide

    def body(j, carry):
        block(pl.multiple_of(2 * j * wide, wide), wide, False)
        block(pl.multiple_of((2 * j + 1) * wide, wide), wide, False)
        return carry

    lax.fori_loop(0, n_wide // 2, body, 0)

    @pl.when(n_wide % 2 == 1)
    def _():
        block(pl.multiple_of((n_wide - 1) * wide, wide), wide, False)

    if per_wide == 2:
        @pl.when(qi % 2 == 1)
        def _():
            block(pl.multiple_of((qi - 1) * tq, tq), tq, False)

    block(pl.multiple_of(qi * tq, tq), tq, True)

    lam = lam_ref[...]
    lam_val = (jnp.exp(jnp.sum(lam[0:1] * lam[1:2], axis=1, keepdims=True))
               - jnp.exp(jnp.sum(lam[2:3] * lam[3:4], axis=1, keepdims=True)) + lambda_init)
    for h in range(heads):
        acc = acc_s[h]
        o12 = acc[:, 0:dv] / acc[:, dv:2 * dv]
        o = o12[0:tq] - lam_val * o12[tq:rows]
        o = o * lax.rsqrt(jnp.mean(o * o, axis=-1, keepdims=True) + RMS_EPS) * sg_ref[...]
        o_ref[:, h * dv:(h + 1) * dv] = (o * (1.0 - lambda_init)).astype(BF16)


def _diff_attn(q, k, v, lam, subln_g, batch, seq, tq, lambda_init):
    t, d = q.shape
    nq = seq // tq
    heads = ATTN_HEADS_PER_STEP
    w = heads * DIFF_DV
    return pl.pallas_call(
        functools.partial(_diff_attn_kernel, tq=tq, lambda_init=lambda_init),
        grid=(batch, DIFF_HEADS // heads, nq),
        in_specs=[pl.BlockSpec((tq, w), lambda b, h, i: (b * nq + i, h)),
                  pl.BlockSpec((seq, w), lambda b, h, i: (b, h)),
                  pl.BlockSpec((seq, w), lambda b, h, i: (b, h)),
                  pl.BlockSpec(lam.shape, lambda b, h, i: (0, 0)),
                  pl.BlockSpec((1, DIFF_DV), lambda b, h, i: (0, 0))],
        out_specs=pl.BlockSpec((tq, w), lambda b, h, i: (b * nq + i, h)),
        out_shape=jax.ShapeDtypeStruct((t, d), BF16),
        scratch_shapes=[pltpu.VMEM((heads, 2 * tq, LANES), F32),
                        pltpu.VMEM((heads, 2 * tq, 2 * DIFF_DV), F32)],
        compiler_params=_cparams("parallel", "parallel", "arbitrary"),
        name="diff_attn",
    )(q, k, v, lam, subln_g)


def kernel(x, a_w_in, a_lb, a_norm_g, a_w_out, kv_w, b_w_q, b_lam, b_subln_g, b_w_out,
           ln1_g, ln1_b, ln2_g, ln2_b, router_w, router_b, moe_w_gate, moe_w_up, moe_w_down):
    batch, seq, d = x.shape
    t = batch * seq
    tm = min(512, t)
    xt = x.reshape(t, d)

    rw = jnp.pad(router_w, ((0, 0), (0, LANES - N_EXPERTS)))
    rw_hi = rw.astype(BF16)
    rw_cat = jnp.concatenate([rw_hi, (rw - rw_hi.astype(F32)).astype(BF16)], axis=1)
    rb_col = router_b.reshape(N_EXPERTS, 1)
    rw_rows = router_w.T.reshape(N_EXPERTS, 1, d)
    w_qkv = jnp.concatenate([b_w_q[0], kv_w], axis=1).astype(BF16)

    qkv = None
    for layer in range(DEPTH):
        if layer < N_A_LAYERS:
            q, k, v, g, lf = _hgrn_in(xt, a_w_in[layer].astype(BF16), a_lb, layer, tm)
            o = _hgrn_scan(q, k, v, lf, g, a_norm_g[layer], batch, seq, min(512, seq))
            w_out = a_w_out[layer]
        else:
            j = layer - N_A_LAYERS
            lambda_init = 0.8 - 0.6 * math.exp(-0.3 * layer)
            o = _diff_attn(*qkv, b_lam[j], b_subln_g[j].reshape(1, DIFF_DV), batch, seq,
                           min(ATTN_TQ, seq), lambda_init)
            w_out = b_w_out[j]
        x1, cls = _post_mix(xt, o, w_out.astype(BF16), ln1_g[layer].reshape(1, d),
                            ln1_b[layer].reshape(1, d), rw_cat, rb_col, tm)
        outs = _moe_layer(x1, cls, rw_rows, moe_w_gate, moe_w_up, moe_w_down, layer,
                          ln2_g[layer].reshape(1, d), ln2_b[layer].reshape(1, d),
                          w_qkv if layer + 1 == N_A_LAYERS else None)
        xt, qkv = outs[0], outs[1:]
    return xt.reshape(batch, seq, d)
```
